```python
import jax, jax.numpy as jnp
from jax import lax
import numpy as np

D_MODEL = 1024
BATCH = 8
SEQ = 4096
DEPTH = 4

CTX_LEN = 256
GRID_W = 64
N_MIXERS = 2
N_ATTN_LAYERS = (DEPTH + 1) // 2
N_GLA_LAYERS = DEPTH // 2
HEAD_DIM = 128
N_HEADS = D_MODEL // HEAD_DIM
N_KV_HEADS = 2
Q_PER_KV = N_HEADS // N_KV_HEADS
Q_BLOCK = 128
ROPE_THETA = 10000.0
ROPE_AXIS_DIM = HEAD_DIM // 2
GLA_HEADS = 4
GLA_DK = D_MODEL // 2 // GLA_HEADS
GLA_DV = D_MODEL // GLA_HEADS
GLA_GATE_RANK = 16
GLA_GATE_TAU = 16.0
GLA_CHUNK = 64
D_FF = 2816
N_MOD = 9
EPS = 1e-6

kernel_name = 'hybrid_gqa_gla_macaron_dit'


def rms_norm(x, g):
    xf = x.astype(jnp.float32)
    y = xf * lax.rsqrt(jnp.mean(xf * xf, axis=-1, keepdims=True) + EPS)
    return (y * g.astype(jnp.float32)).astype(x.dtype)


def modulate(h, g, shift, scale):
    return rms_norm(h, g) * (1 + scale) + shift


def swiglu(u, w_in, w_out):
    gate, up = jnp.split(u @ w_in, 2, axis=-1)
    return (jax.nn.silu(gate) * up) @ w_out


def axial_rope_tables(rows):
    row = jnp.repeat(jnp.arange(rows, dtype=jnp.float32), GRID_W)
    col = jnp.tile(jnp.arange(GRID_W, dtype=jnp.float32), rows)
    inv = ROPE_THETA ** (-jnp.arange(0, ROPE_AXIS_DIM, 2, dtype=jnp.float32) / ROPE_AXIS_DIM)
    ang_r = row[:, None] * inv[None, :]
    ang_c = col[:, None] * inv[None, :]
    return (jnp.cos(ang_r), jnp.sin(ang_r), jnp.cos(ang_c), jnp.sin(ang_c))


def rotate(x, cos, sin):
    x1, x2 = jnp.split(x, 2, axis=-1)
    cos = cos[None, :, None, :]
    sin = sin[None, :, None, :]
    return jnp.concatenate([x1 * cos - x2 * sin, x2 * cos + x1 * sin], axis=-1)


def apply_axial_rope(x, rope):
    cos_r, sin_r, cos_c, sin_c = rope
    xr, xc = jnp.split(x, 2, axis=-1)
    return jnp.concatenate([rotate(xr, cos_r, sin_r), rotate(xc, cos_c, sin_c)], axis=-1).astype(x.dtype)


def attend(q, k, v):
    s = jnp.einsum('bqkgd,bskd->bkgqs', q, k).astype(jnp.float32) * (HEAD_DIM ** -0.5)
    p = jax.nn.softmax(s, axis=-1).astype(v.dtype)
    return jnp.einsum('bkgqs,bskd->bqkgd', p, v)


def gqa_axial_attention(uc, ux, w_qkv, q_gain, k_gain, w_o, rope, want_ctx):
    def project(u):
        b_, t_, _ = u.shape
        q, k, v = jnp.split(u @ w_qkv, [N_HEADS * HEAD_DIM, (N_HEADS + N_KV_HEADS) * HEAD_DIM], axis=-1)
        q = rms_norm(q.reshape(b_, t_, N_HEADS, HEAD_DIM), q_gain)
        k = rms_norm(k.reshape(b_, t_, N_KV_HEADS, HEAD_DIM), k_gain)
        v = v.reshape(b_, t_, N_KV_HEADS, HEAD_DIM)
        return q, k, v

    qc, kc, vc = project(uc)
    qx, kx, vx = project(ux)
    qx = apply_axial_rope(qx, rope)
    kx = apply_axial_rope(kx, rope)
    k_all = jnp.concatenate([kc, kx], axis=1)
    v_all = jnp.concatenate([vc, vx], axis=1)
    b_, s_ = qx.shape[:2]
    n_blk = s_ // Q_BLOCK
    q_blocks = jnp.moveaxis(qx.reshape(b_, n_blk, Q_BLOCK, N_KV_HEADS, Q_PER_KV, HEAD_DIM), 1, 0)
    ox = lax.map(lambda qb: attend(qb, k_all, v_all), q_blocks)
    ox = jnp.moveaxis(ox, 0, 1).reshape(b_, s_, N_HEADS * HEAD_DIM) @ w_o
    oc = None
    if want_ctx:
        l_ = qc.shape[1]
        oc = attend(qc.reshape(b_, l_, N_KV_HEADS, Q_PER_KV, HEAD_DIM), kc, vc)
        oc = oc.reshape(b_, l_, N_HEADS * HEAD_DIM) @ w_o
    return ox, oc


def gla_chunked(q, k, v, log_a, state0):
    b_, t_, h_, _ = q.shape
    dv = v.shape[-1]
    n = t_ // GLA_CHUNK

    def blocks(a):
        return jnp.moveaxis(a.astype(jnp.float32).reshape(b_, n, GLA_CHUNK, h_, a.shape[-1]), 1, 0)

    mask = jnp.tril(jnp.ones((GLA_CHUNK, GLA_CHUNK), dtype=bool))

    def step(state, inp):
        qc, kc, vc, gc = inp
        cum = jnp.cumsum(gc, axis=1)
        last = cum[:, -1]
        q_dec = qc * jnp.exp(cum)
        k_inv = kc * jnp.exp(-cum)
        k_end = kc * jnp.exp(last[:, None] - cum)
        scores = jnp.where(mask, jnp.einsum('bihd,bjhd->bhij', q_dec, k_inv), 0.0)
        o = jnp.einsum('bhij,bjhv->bihv', scores, vc) + jnp.einsum('bihd,bhdv->bihv', q_dec, state)
        state = jnp.exp(last)[..., None] * state + jnp.einsum('bjhd,bjhv->bhdv', k_end, vc)
        return state, o

    state, o = lax.scan(step, state0, (blocks(q), blocks(k), blocks(v), blocks(log_a)))
    o = jnp.moveaxis(o, 0, 1).reshape(b_, t_, h_, dv)
    return o, state


def bidirectional_gla(uc, ux, w_in, wa1, wa2, ba, o_gain, w_o, want_ctx):
    hk = GLA_HEADS * GLA_DK
    hv = GLA_HEADS * GLA_DV

    def project(u):
        b_, t_, _ = u.shape
        q, k, v, r = jnp.split(u @ w_in, [hk, 2 * hk, 2 * hk + hv], axis=-1)
        q = q.reshape(b_, t_, GLA_HEADS, GLA_DK) * (GLA_DK ** -0.5)
        k = k.reshape(b_, t_, GLA_HEADS, GLA_DK)
        v = v.reshape(b_, t_, GLA_HEADS, GLA_DV)
        g_fwd = jax.nn.log_sigmoid(((u @ wa1[0]) @ wa2[0] + ba[0]).astype(jnp.float32)) / GLA_GATE_TAU
        g_bwd = jax.nn.log_sigmoid(((u @ wa1[1]) @ wa2[1] + ba[1]).astype(jnp.float32)) / GLA_GATE_TAU
        return q, k, v, r, g_fwd.reshape(b_, t_, GLA_HEADS, GLA_DK), g_bwd.reshape(b_, t_, GLA_HEADS, GLA_DK)

    def flip(a):
        return jnp.flip(a, axis=1)

    qc, kc, vc, rc, gcf, gcb = project(uc)
    qx, kx, vx, rx, gxf, gxb = project(ux)
    s0 = jnp.zeros((uc.shape[0], GLA_HEADS, GLA_DK, GLA_DV), jnp.float32)
    oc_f, sc_f = gla_chunked(qc, kc, vc, gcf, s0)
    oc_b, sc_b = gla_chunked(flip(qc), flip(kc), flip(vc), flip(gcb), s0)
    ox_f, _ = gla_chunked(qx, kx, vx, gxf, sc_f)
    ox_b, _ = gla_chunked(flip(qx), flip(kx), flip(vx), flip(gxb), sc_b)

    def readout(o, r):
        b_, t_ = o.shape[:2]
        o = rms_norm(o, o_gain.reshape(GLA_HEADS, GLA_DV)).reshape(b_, t_, hv).astype(r.dtype)
        return (o * jax.nn.silu(r)) @ w_o

    ox = readout(ox_f + flip(ox_b), rx)
    oc = readout(oc_f + flip(oc_b), rc) if want_ctx else None
    return ox, oc


def setup_inputs(seed: int = 0) -> dict:
    key = jax.random.key(seed)
    ks = jax.random.split(key, 20)

    def normal(k, shape, scale=1.0):
        return jax.random.normal(k, shape, jnp.float32) * scale

    return {
        'x': normal(ks[0], (BATCH, SEQ, D_MODEL)),
        'c': normal(ks[1], (BATCH, D_MODEL)),
        'ctx': normal(ks[2], (BATCH, CTX_LEN, D_MODEL)),
        'c_ctx': normal(ks[3], (D_MODEL,)),
        'ada_w': normal(ks[4], (DEPTH, D_MODEL, N_MOD * D_MODEL), 0.5 * D_MODEL ** -0.5),
        'ada_b': normal(ks[5], (DEPTH, N_MOD * D_MODEL), 0.02),
        'norm_pre': 1.0 + normal(ks[6], (DEPTH, 3, D_MODEL), 0.02),
        'norm_post': 1.0 + normal(ks[7], (DEPTH, 3, D_MODEL), 0.02),
        'ffn_w_in': normal(ks[8], (DEPTH, 2, D_MODEL, 2 * D_FF), D_MODEL ** -0.5),
        'ffn_w_out': normal(ks[9], (DEPTH, 2, D_FF, D_MODEL), D_FF ** -0.5),
        'attn_w_qkv': normal(ks[10], (N_ATTN_LAYERS, D_MODEL, (N_HEADS + 2 * N_KV_HEADS) * HEAD_DIM), D_MODEL ** -0.5),
        'attn_q_gain': 1.0 + normal(ks[11], (N_ATTN_LAYERS, HEAD_DIM), 0.02),
        'attn_k_gain': 1.0 + normal(ks[12], (N_ATTN_LAYERS, HEAD_DIM), 0.02),
        'attn_w_o': normal(ks[13], (N_ATTN_LAYERS, N_HEADS * HEAD_DIM, D_MODEL), (N_HEADS * HEAD_DIM) ** -0.5),
        'gla_w_in': normal(ks[14], (N_GLA_LAYERS, D_MODEL, 2 * GLA_HEADS * GLA_DK + 2 * GLA_HEADS * GLA_DV), D_MODEL ** -0.5),
        'gla_wa1': normal(ks[15], (N_GLA_LAYERS, 2, D_MODEL, GLA_GATE_RANK), D_MODEL ** -0.5),
        'gla_wa2': normal(ks[16], (N_GLA_LAYERS, 2, GLA_GATE_RANK, GLA_HEADS * GLA_DK), GLA_GATE_RANK ** -0.5),
        'gla_ba': normal(ks[17], (N_GLA_LAYERS, 2, GLA_HEADS * GLA_DK), 0.1),
        'gla_o_gain': 1.0 + normal(ks[18], (N_GLA_LAYERS, GLA_HEADS * GLA_DV), 0.02),
        'gla_w_o': normal(ks[19], (N_GLA_LAYERS, GLA_HEADS * GLA_DV, D_MODEL), (GLA_HEADS * GLA_DV) ** -0.5),
    }


def reference(x, c, ctx, c_ctx, ada_w, ada_b, norm_pre, norm_post, ffn_w_in, ffn_w_out,
              attn_w_qkv, attn_q_gain, attn_k_gain, attn_w_o,
              gla_w_in, gla_wa1, gla_wa2, gla_ba, gla_o_gain, gla_w_o):
    b_, n_tok, _ = x.shape
    ROWS = n_tok // GRID_W
    rope = axial_rope_tables(ROWS)
    silu_c = jax.nn.silu(c)
    silu_cc = jax.nn.silu(c_ctx)
    hx, hc = x, ctx
    for i in range(DEPTH):
        last = i == DEPTH - 1
        mod_x = jnp.moveaxis((silu_c @ ada_w[i] + ada_b[i]).reshape(b_, N_MOD, 1, D_MODEL), 1, 0)
        mod_c = (silu_cc @ ada_w[i] + ada_b[i]).reshape(N_MOD, D_MODEL)
        g_pre, g_post = norm_pre[i], norm_post[i]

        def ffn_half(h, mod, j):
            u = modulate(h, g_pre[j], mod[3 * j], mod[3 * j + 1])
            y = swiglu(u, ffn_w_in[i, j // 2], ffn_w_out[i, j // 2])
            return h + 0.5 * mod[3 * j + 2] * rms_norm(y, g_post[j])

        hx = ffn_half(hx, mod_x, 0)
        hc = ffn_half(hc, mod_c, 0)
        ux = modulate(hx, g_pre[1], mod_x[3], mod_x[4])
        uc = modulate(hc, g_pre[1], mod_c[3], mod_c[4])
        m = i // N_MIXERS
        if i % N_MIXERS == 0:
            ox, oc = gqa_axial_attention(uc, ux, attn_w_qkv[m], attn_q_gain[m], attn_k_gain[m], attn_w_o[m],
                                         rope, not last)
        else:
            ox, oc = bidirectional_gla(uc, ux, gla_w_in[m], gla_wa1[m], gla_wa2[m], gla_ba[m], gla_o_gain[m],
                                       gla_w_o[m], not last)
        hx = hx + mod_x[5] * rms_norm(ox, g_post[1])
        hx = ffn_half(hx, mod_x, 2)
        if not last:
            hc = hc + mod_c[5] * rms_norm(oc, g_post[1])
            hc = ffn_half(hc, mod_c, 2)
    return hx
```

```python
import functools

import jax
import jax.numpy as jnp
from jax import lax
from jax.experimental import pallas as pl
from jax.experimental.pallas import tpu as pltpu

D_MODEL = 1024
DEPTH = 4
GRID_W = 64
N_MOD = 9
EPS = 1e-6
HEAD_DIM = 128
N_HEADS = 8
N_KV_HEADS = 2
Q_PER_KV = N_HEADS // N_KV_HEADS
ROPE_THETA = 10000.0
ROPE_AXIS_DIM = HEAD_DIM // 2
ROPE_HALF = ROPE_AXIS_DIM // 2
GLA_HEADS = 4
GLA_DK = 128
GLA_DV = 256
GLA_HK = GLA_HEADS * GLA_DK
GLA_HV = GLA_HEADS * GLA_DV
GLA_GATE_RANK = 16
GLA_GATE_TAU = 16.0
GLA_CHUNK = 64
D_FF = 2816

BF16 = jnp.bfloat16
F32 = jnp.float32

MOD_ROWS = 16
ADA_TN = 2304
TOKEN_TILE = 512
ATTN_TQ = 128
SCAN_TOKENS = 256
VMEM_LIMIT_BYTES = 56 * 1024 * 1024


def _cparams(*sem):
    return pltpu.CompilerParams(dimension_semantics=sem, vmem_limit_bytes=VMEM_LIMIT_BYTES)


def _resident(shape):
    nd = len(shape)
    return pl.BlockSpec(shape, lambda *_: (0,) * nd, pipeline_mode=pl.Buffered(1))


def _dot(a, b):
    return jnp.dot(a, b, preferred_element_type=F32)


def _dot_nt(a, b):
    return lax.dot_general(a, b, (((1,), (1,)), ((), ())), preferred_element_type=F32)


def _dot_tn(a, b):
    return lax.dot_general(a, b, (((0,), (0,)), ((), ())), preferred_element_type=F32)


def _rms(x, g):
    return x * lax.rsqrt(jnp.mean(x * x, axis=-1, keepdims=True) + EPS) * g


def _silu(x):
    return x * jax.nn.sigmoid(x)


def _modulated(h, mod_ref, gpre_ref, j):
    shift = mod_ref[3 * j:3 * j + 1, :]
    scale = mod_ref[3 * j + 1:3 * j + 2, :]
    return (_rms(h, gpre_ref[j:j + 1, :]) * (1.0 + scale) + shift).astype(BF16)


def _ada_kernel(c_ref, w_ref, b_ref, o_ref):
    s = _silu(c_ref[...]).astype(BF16)
    o_ref[...] = _dot(s, w_ref[...].astype(BF16)) + b_ref[...]


def _ada(cc, ada_w, ada_b):
    n = N_MOD * D_MODEL
    return pl.pallas_call(
        _ada_kernel,
        grid=(DEPTH, n // ADA_TN),
        in_specs=[
            _resident((MOD_ROWS, D_MODEL)),
            pl.BlockSpec((None, D_MODEL, ADA_TN), lambda l, t: (l, 0, t)),
            pl.BlockSpec((None, 1, ADA_TN), lambda l, t: (l, 0, t)),
        ],
        out_specs=pl.BlockSpec((None, MOD_ROWS, ADA_TN), lambda l, t: (l, 0, t)),
        out_shape=jax.ShapeDtypeStruct((DEPTH, MOD_ROWS, n), F32),
        compiler_params=_cparams("parallel", "parallel"),
        name="ada_mod",
    )(cc, ada_w, ada_b.reshape(DEPTH, 1, n))


def _ffn_kernel(j, h_ref, mod_ref, gpre_ref, gpost_ref, win_ref, wout_ref, o_ref):
    h = h_ref[...]
    u = _modulated(h, mod_ref, gpre_ref, j)
    gu = _dot(u, win_ref[...])
    a = (_silu(gu[:, :D_FF]) * gu[:, D_FF:]).astype(BF16)
    y = _dot(a, wout_ref[...])
    gate = mod_ref[3 * j + 2:3 * j + 3, :]
    o_ref[...] = h + 0.5 * gate * _rms(y, gpost_ref[j:j + 1, :])


def _tok_spec(width, tm=TOKEN_TILE):
    return pl.BlockSpec((None, tm, width), lambda b, t: (b, t, 0))


def _mod_spec():
    return pl.BlockSpec((None, N_MOD, D_MODEL), lambda b, t: (b, 0, 0))


def _ffn(h, mod, gpre, gpost, w_in, w_out, j):
    nb, s, _ = h.shape
    return pl.pallas_call(
        functools.partial(_ffn_kernel, j),
        grid=(nb, s // TOKEN_TILE),
        in_specs=[
            _tok_spec(D_MODEL), _mod_spec(),
            _resident((3, D_MODEL)), _resident((3, D_MODEL)),
            _resident((D_MODEL, 2 * D_FF)), _resident((D_FF, D_MODEL)),
        ],
        out_specs=_tok_spec(D_MODEL),
        out_shape=jax.ShapeDtypeStruct(h.shape, F32),
        compiler_params=_cparams("parallel", "parallel"),
        name=f"ffn_half{j}",
    )(h, mod, gpre, gpost, w_in, w_out)


def _rope_partner(x):
    lane = lax.broadcasted_iota(jnp.int32, x.shape, 1)
    return jnp.where((lane & ROPE_HALF) == 0,
                     pltpu.roll(x, HEAD_DIM - ROPE_HALF, 1), pltpu.roll(x, ROPE_HALF, 1))


def _attn_prep_kernel(use_rope, h_ref, mod_ref, gpre_ref, w_ref, qg_ref, kg_ref, *refs):
    if use_rope:
        cos_ref, sin_ref, q_ref, k_ref, v_ref = refs
    else:
        q_ref, k_ref, v_ref = refs
    u = _modulated(h_ref[...], mod_ref, gpre_ref, 1)
    qkv = _dot(u, w_ref[...])

    def head(col, gain_ref):
        x = _rms(qkv[:, col:col + HEAD_DIM], gain_ref[...])
        if use_rope:
            x = x * cos_ref[...] + _rope_partner(x) * sin_ref[...]
        return x.astype(BF16)

    for hd in range(N_HEADS):
        q_ref[:, hd * HEAD_DIM:(hd + 1) * HEAD_DIM] = head(hd * HEAD_DIM, qg_ref)
    k0 = N_HEADS * HEAD_DIM
    for hd in range(N_KV_HEADS):
        k_ref[:, hd * HEAD_DIM:(hd + 1) * HEAD_DIM] = head(k0 + hd * HEAD_DIM, kg_ref)
    v0 = k0 + N_KV_HEADS * HEAD_DIM
    v_ref[...] = qkv[:, v0:v0 + N_KV_HEADS * HEAD_DIM].astype(BF16)


def _attn_prep(h, mod, gpre, w_qkv, q_gain, k_gain, rope):
    nb, s, _ = h.shape
    kvw = N_KV_HEADS * HEAD_DIM
    in_specs = [
        _tok_spec(D_MODEL), _mod_spec(), _resident((3, D_MODEL)),
        _resident((D_MODEL, D_MODEL + 2 * kvw)),
        _resident((1, HEAD_DIM)), _resident((1, HEAD_DIM)),
    ]
    args = [h, mod, gpre, w_qkv, q_gain, k_gain]
    if rope is not None:
        in_specs += [pl.BlockSpec((TOKEN_TILE, HEAD_DIM), lambda b, t: (t, 0))] * 2
        args += list(rope)
    return pl.pallas_call(
        functools.partial(_attn_prep_kernel, rope is not None),
        grid=(nb, s // TOKEN_TILE),
        in_specs=in_specs,
        out_specs=[_tok_spec(D_MODEL), _tok_spec(kvw), _tok_spec(kvw)],
        out_shape=[jax.ShapeDtypeStruct((nb, s, D_MODEL), BF16),
                   jax.ShapeDtypeStruct((nb, s, kvw), BF16),
                   jax.ShapeDtypeStruct((nb, s, kvw), BF16)],
        compiler_params=_cparams("parallel", "parallel"),
        name="attn_prep_rope" if rope is not None else "attn_prep_ctx",
    )(*args)


def _attention_kernel(n_seg, q_ref, *refs):
    kv_refs, o_ref = refs[:2 * n_seg], refs[2 * n_seg]
    q = q_ref[...]
    qs = jnp.concatenate([q[:, g * HEAD_DIM:(g + 1) * HEAD_DIM] for g in range(Q_PER_KV)], axis=0)
    scores = [_dot_nt(qs, kv_refs[2 * i][...]) for i in range(n_seg)]
    m = functools.reduce(jnp.maximum, [jnp.max(s, axis=-1, keepdims=True) for s in scores])
    probs = [jnp.exp((s - m) * (HEAD_DIM ** -0.5)) for s in scores]
    denom = functools.reduce(jnp.add, [jnp.sum(p, axis=-1, keepdims=True) for p in probs])
    out = functools.reduce(
        jnp.add, [_dot(p.astype(BF16), kv_refs[2 * i + 1][...]) for i, p in enumerate(probs)])
    out = out / denom
    tq = q.shape[0]
    for g in range(Q_PER_KV):
        o_ref[:, g * HEAD_DIM:(g + 1) * HEAD_DIM] = out[g * tq:(g + 1) * tq, :].astype(BF16)


def _attention(q, segments, tq):
    b, s, _ = q.shape
    gw = Q_PER_KV * HEAD_DIM
    in_specs = [pl.BlockSpec((None, tq, gw), lambda bi, kh, t: (bi, t, kh))]
    args = [q]
    for k, v in segments:
        spec = pl.BlockSpec((None, k.shape[1], HEAD_DIM), lambda bi, kh, t: (bi, 0, kh))
        in_specs += [spec, spec]
        args += [k, v]
    return pl.pallas_call(
        functools.partial(_attention_kernel, len(segments)),
        grid=(b, N_KV_HEADS, s // tq),
        in_specs=in_specs,
        out_specs=pl.BlockSpec((None, tq, gw), lambda bi, kh, t: (bi, t, kh)),
        out_shape=jax.ShapeDtypeStruct(q.shape, BF16),
        compiler_params=_cparams("parallel", "parallel", "parallel"),
        name=f"attention_{len(segments)}seg",
    )(*args)


def _mix_residual(z, w_ref, h_ref, mod_ref, gpost_ref, out_ref):
    y = _dot(z, w_ref[...])
    out_ref[...] = h_ref[...] + mod_ref[5:6, :] * _rms(y, gpost_ref[1:2, :])


def _attn_out_kernel(o_ref, w_ref, h_ref, mod_ref, gpost_ref, out_ref):
    _mix_residual(o_ref[...], w_ref, h_ref, mod_ref, gpost_ref, out_ref)


def _attn_out(o, w_o, h, mod, gpost):
    nb, s, _ = h.shape
    return pl.pallas_call(
        _attn_out_kernel,
        grid=(nb, s // TOKEN_TILE),
        in_specs=[_tok_spec(D_MODEL), _resident((D_MODEL, D_MODEL)), _tok_spec(D_MODEL),
                  _mod_spec(), _resident((3, D_MODEL))],
        out_specs=_tok_spec(D_MODEL),
        out_shape=jax.ShapeDtypeStruct(h.shape, F32),
        compiler_params=_cparams("parallel", "parallel"),
        name="attn_out",
    )(o, w_o, h, mod, gpost)


def _gla_out_kernel(of_ref, ob_ref, r_ref, og_ref, w_ref, h_ref, mod_ref, gpost_ref, out_ref):
    o = of_ref[...] + ob_ref[...]
    normed = jnp.concatenate(
        [_rms(o[:, hd * GLA_DV:(hd + 1) * GLA_DV], og_ref[:, hd * GLA_DV:(hd + 1) * GLA_DV])
         for hd in range(GLA_HEADS)], axis=-1)
    z = (normed * _silu(r_ref[...].astype(F32))).astype(BF16)
    _mix_residual(z, w_ref, h_ref, mod_ref, gpost_ref, out_ref)


def _gla_out(o_f, o_b, r, o_gain, w_o, h, mod, gpost):
    nb, s, _ = h.shape
    return pl.pallas_call(
        _gla_out_kernel,
        grid=(nb, s // TOKEN_TILE),
        in_specs=[_tok_spec(GLA_HV), _tok_spec(GLA_HV), _tok_spec(GLA_HV),
                  _resident((1, GLA_HV)), _resident((GLA_HV, D_MODEL)), _tok_spec(D_MODEL),
                  _mod_spec(), _resident((3, D_MODEL))],
        out_specs=_tok_spec(D_MODEL),
        out_shape=jax.ShapeDtypeStruct(h.shape, F32),
        compiler_params=_cparams("parallel", "parallel"),
        name="gla_out",
    )(o_f, o_b, r, o_gain, w_o, h, mod, gpost)


def _chunk_cumsum(g, reverse):
    rows = g.shape[0]
    pos = lax.broadcasted_iota(jnp.int32, g.shape, 0) % GLA_CHUNK
    x = g
    step = 1
    while step < GLA_CHUNK:
        if reverse:
            x = x + jnp.where(pos < GLA_CHUNK - step, pltpu.roll(x, rows - step, 0), 0.0)
        else:
            x = x + jnp.where(pos >= step, pltpu.roll(x, step, 0), 0.0)
        step *= 2
    return x


def _gla_prep_kernel(h_ref, mod_ref, gpre_ref, win_ref, wa1_ref, wa2_ref, ba_ref,
                     qdf_ref, kif_ref, kef_ref, qdb_ref, kib_ref, keb_ref, v_ref, r_ref, dec_ref,
                     cum_ref):
    u = _modulated(h_ref[...], mod_ref, gpre_ref, 1)
    proj = _dot(u, win_ref[...])
    q = proj[:, :GLA_HK] * (GLA_DK ** -0.5)
    k = proj[:, GLA_HK:2 * GLA_HK]
    v_ref[...] = proj[:, 2 * GLA_HK:2 * GLA_HK + GLA_HV].astype(BF16)
    r_ref[...] = proj[:, 2 * GLA_HK + GLA_HV:].astype(BF16)
    low = _dot(u, wa1_ref[...]).astype(BF16)
    gates = jax.nn.log_sigmoid(_dot(low, wa2_ref[...]) + ba_ref[...]) / GLA_GATE_TAU
    tm = q.shape[0]
    n_chunks = tm // GLA_CHUNK
    outs = ((qdf_ref, kif_ref, kef_ref), (qdb_ref, kib_ref, keb_ref))
    for d, (qd_ref, ki_ref, ke_ref) in enumerate(outs):
        cum = _chunk_cumsum(gates[:, d * GLA_HK:(d + 1) * GLA_HK], reverse=(d == 1))
        qd_ref[...] = (q * jnp.exp(cum)).astype(BF16)
        ki_ref[...] = (k * jnp.exp(-cum)).astype(BF16)
        edge = GLA_CHUNK - 1 if d == 0 else 0
        totals = []
        for hd in range(GLA_HEADS):
            cum_ref[hd] = cum[:, hd * GLA_DK:(hd + 1) * GLA_DK]
            totals.append(cum_ref[hd, pl.ds(edge, n_chunks, stride=GLA_CHUNK), :])
        total = jnp.concatenate(totals, axis=-1)
        dec_ref[:, d * GLA_HK:(d + 1) * GLA_HK] = jnp.exp(total)
        spread = jnp.broadcast_to(total[:, None, :], (n_chunks, GLA_CHUNK, GLA_HK)).reshape(tm, GLA_HK)
        ke_ref[...] = (k * jnp.exp(spread - cum)).astype(BF16)


def _gla_prep(h, mod, gpre, w_in, wa1, wa2, ba):
    nb, s, _ = h.shape
    nck = TOKEN_TILE // GLA_CHUNK
    half = jax.ShapeDtypeStruct((nb, s, GLA_HK), BF16)
    full = jax.ShapeDtypeStruct((nb, s, GLA_HV), BF16)
    return pl.pallas_call(
        _gla_prep_kernel,
        grid=(nb, s // TOKEN_TILE),
        in_specs=[_tok_spec(D_MODEL), _mod_spec(), _resident((3, D_MODEL)),
                  _resident(w_in.shape), _resident(wa1.shape), _resident(wa2.shape),
                  _resident(ba.shape)],
        out_specs=[_tok_spec(GLA_HK)] * 6 + [_tok_spec(GLA_HV)] * 2
        + [pl.BlockSpec((None, nck, 2 * GLA_HK), lambda b, t: (b, t, 0))],
        out_shape=[half] * 6 + [full] * 2
        + [jax.ShapeDtypeStruct((nb, s // GLA_CHUNK, 2 * GLA_HK), F32)],
        scratch_shapes=[pltpu.VMEM((GLA_HEADS, TOKEN_TILE, GLA_DK), F32)],
        compiler_params=_cparams("parallel", "parallel"),
        name="gla_prep",
    )(h, mod, gpre, w_in, wa1, wa2, ba)


def _gla_scan_kernel(n_blocks, qdf_ref, kif_ref, kef_ref, vf_ref, qdb_ref, kib_ref, keb_ref, vb_ref,
                     dec_ref, s0_ref, of_ref, ob_ref, st_ref):
    t = pl.program_id(1)

    @pl.when(t == 0)
    def _():
        st_ref[...] = s0_ref[...]

    n_chunks = qdf_ref.shape[0] // GLA_CHUNK
    row = lax.broadcasted_iota(jnp.int32, (GLA_CHUNK, GLA_CHUNK), 0)
    col = lax.broadcasted_iota(jnp.int32, (GLA_CHUNK, GLA_CHUNK), 1)
    dirs = ((qdf_ref, kif_ref, kef_ref, vf_ref, of_ref, col <= row),
            (qdb_ref, kib_ref, keb_ref, vb_ref, ob_ref, col >= row))
    for d, (qd_ref, ki_ref, ke_ref, v_ref, o_ref, mask) in enumerate(dirs):
        block = t if d == 0 else n_blocks - 1 - t
        for step in range(n_chunks):
            c = step if d == 0 else n_chunks - 1 - step
            rows = slice(c * GLA_CHUNK, (c + 1) * GLA_CHUNK)
            decay = dec_ref[pl.ds(block * n_chunks + c, 1), :]
            for hd in range(GLA_HEADS):
                kl = slice(hd * GLA_DK, (hd + 1) * GLA_DK)
                vl = slice(hd * GLA_DV, (hd + 1) * GLA_DV)
                qd, vv = qd_ref[rows, kl], v_ref[rows, vl]
                state = st_ref[d * GLA_HEADS + hd]
                scores = jnp.where(mask, _dot_nt(qd, ki_ref[rows, kl]), 0.0)
                o_ref[rows, vl] = _dot(scores.astype(BF16), vv) + _dot_nt(qd, state.astype(BF16))
                st_ref[d * GLA_HEADS + hd] = (
                    decay[:, d * GLA_HK + hd * GLA_DK:d * GLA_HK + (hd + 1) * GLA_DK] * state
                    + _dot_tn(vv, ke_ref[rows, kl]))


def _gla_scan(qdf, kif, kef, qdb, kib, keb, v, dec, state0):
    b, s, _ = v.shape
    n_blocks = s // SCAN_TOKENS
    fwd = lambda bi, t: (bi, t, 0)
    bwd = lambda bi, t: (bi, n_blocks - 1 - t, 0)
    hk = lambda m: pl.BlockSpec((None, SCAN_TOKENS, GLA_HK), m)
    hv = lambda m: pl.BlockSpec((None, SCAN_TOKENS, GLA_HV), m)
    st_spec = pl.BlockSpec((None, 2 * GLA_HEADS, GLA_DV, GLA_DK), lambda bi, t: (bi, 0, 0, 0))
    return pl.pallas_call(
        functools.partial(_gla_scan_kernel, n_blocks),
        grid=(b, n_blocks),
        in_specs=[hk(fwd), hk(fwd), hk(fwd), hv(fwd), hk(bwd), hk(bwd), hk(bwd), hv(bwd),
                  pl.BlockSpec((None, s // GLA_CHUNK, 2 * GLA_HK), lambda bi, t: (bi, 0, 0)),
                  st_spec],
        out_specs=[hv(fwd), hv(bwd), st_spec],
        out_shape=[jax.ShapeDtypeStruct((b, s, GLA_HV), F32)] * 2
        + [jax.ShapeDtypeStruct(state0.shape, F32)],
        compiler_params=_cparams("parallel", "arbitrary"),
        name="gla_scan",
    )(qdf, kif, kef, v, qdb, kib, keb, v, dec, state0)


def _rope_tables(n_tok):
    pos = jnp.arange(n_tok, dtype=jnp.int32)
    inv = ROPE_THETA ** (-jnp.arange(0, ROPE_AXIS_DIM, 2, dtype=F32) / ROPE_AXIS_DIM)
    ang_r = (pos // GRID_W).astype(F32)[:, None] * inv[None, :]
    ang_c = (pos % GRID_W).astype(F32)[:, None] * inv[None, :]
    cos = jnp.concatenate([jnp.cos(ang_r)] * 2 + [jnp.cos(ang_c)] * 2, axis=-1)
    sin = jnp.concatenate([-jnp.sin(ang_r), jnp.sin(ang_r), -jnp.sin(ang_c), jnp.sin(ang_c)], axis=-1)
    return cos, sin


def kernel(x, c, ctx, c_ctx, ada_w, ada_b, norm_pre, norm_post, ffn_w_in, ffn_w_out, attn_w_qkv, attn_q_gain, attn_k_gain, attn_w_o, gla_w_in, gla_wa1, gla_wa2, gla_ba, gla_o_gain, gla_w_o):
    b, n_tok, _ = x.shape
    n_ctx = ctx.shape[1]
    assert b + 1 <= MOD_ROWS and n_tok % TOKEN_TILE == 0 and (b * n_ctx) % TOKEN_TILE == 0

    cc = jnp.concatenate([c, c_ctx[None, :], jnp.zeros((MOD_ROWS - b - 1, D_MODEL), F32)], axis=0)
    mods = _ada(cc, ada_w, ada_b).reshape(DEPTH, MOD_ROWS, N_MOD, D_MODEL)
    rope = _rope_tables(n_tok)

    def per_batch(a):
        return a.reshape(b, n_ctx, a.shape[-1])

    hx = x
    hc = ctx.reshape(1, b * n_ctx, D_MODEL)
    for i in range(DEPTH):
        last = i == DEPTH - 1
        mod_x, mod_c = mods[i, :b], mods[i, b:b + 1]
        gpre, gpost = norm_pre[i], norm_post[i]
        w_in = ffn_w_in[i].astype(BF16)
        w_out = ffn_w_out[i].astype(BF16)

        hx = _ffn(hx, mod_x, gpre, gpost, w_in[0], w_out[0], 0)
        hc = _ffn(hc, mod_c, gpre, gpost, w_in[0], w_out[0], 0)

        m = i // 2
        if i % 2 == 0:
            w_qkv = attn_w_qkv[m].astype(BF16)
            w_o = attn_w_o[m].astype(BF16)
            qg, kg = attn_q_gain[m][None, :], attn_k_gain[m][None, :]
            qx, kx, vx = _attn_prep(hx, mod_x, gpre, w_qkv, qg, kg, rope)
            qc, kc, vc = (per_batch(a) for a in _attn_prep(hc, mod_c, gpre, w_qkv, qg, kg, None))
            ox = _attention(qx, [(kc, vc), (kx, vx)], ATTN_TQ)
            hx = _attn_out(ox, w_o, hx, mod_x, gpost)
            if not last:
                oc = _attention(qc, [(kc, vc)], n_ctx)
                hc = _attn_out(oc.reshape(hc.shape), w_o, hc, mod_c, gpost)
        else:
            w_gin = gla_w_in[m].astype(BF16)
            w_o = gla_w_o[m].astype(BF16)
            wa1 = jnp.concatenate([gla_wa1[m, 0], gla_wa1[m, 1]], axis=-1).astype(BF16)
            zeros = jnp.zeros((GLA_GATE_RANK, GLA_HK), F32)
            wa2 = jnp.concatenate([jnp.concatenate([gla_wa2[m, 0], zeros], axis=-1),
                                   jnp.concatenate([zeros, gla_wa2[m, 1]], axis=-1)], axis=0).astype(BF16)
            ba = gla_ba[m].reshape(1, 2 * GLA_HK)
            o_gain = gla_o_gain[m][None, :]
            px = _gla_prep(hx, mod_x, gpre, w_gin, wa1, wa2, ba)
            pc = _gla_prep(hc, mod_c, gpre, w_gin, wa1, wa2, ba)
            pc = [per_batch(a) for a in pc[:8]] + [pc[8].reshape(b, n_ctx // GLA_CHUNK, 2 * GLA_HK)]
            zero_state = jnp.zeros((b, 2 * GLA_HEADS, GLA_DV, GLA_DK), F32)
            oc_f, oc_b, ctx_state = _gla_scan(*pc[:7], pc[8], zero_state)
            ox_f, ox_b, _ = _gla_scan(*px[:7], px[8], ctx_state)
            hx = _gla_out(ox_f, ox_b, px[7], o_gain, w_o, hx, mod_x, gpost)
            if not last:
                hc = _gla_out(oc_f.reshape(1, b * n_ctx, GLA_HV), oc_b.reshape(1, b * n_ctx, GLA_HV),
                              pc[7].reshape(1, b * n_ctx, GLA_HV), o_gain, w_o, hc, mod_c, gpost)

        hx = _ffn(hx, mod_x, gpre, gpost, w_in[1], w_out[1], 2)
        if not last:
            hc = _ffn(hc, mod_c, gpre, gpost, w_in[1], w_out[1], 2)
    return hx
```

```python
import functools

import jax
import jax.numpy as jnp
from jax import lax
from jax.experimental import pallas as pl
from jax.experimental.pallas import tpu as pltpu

D_MODEL = 1024
DEPTH = 4
GRID_W = 64
N_MOD = 9
EPS = 1e-6
HEAD_DIM = 128
N_HEADS = 8
N_KV_HEADS = 2
Q_PER_KV = N_HEADS // N_KV_HEADS
ROPE_THETA = 10000.0
ROPE_AXIS_DIM = HEAD_DIM // 2
ROPE_HALF = ROPE_AXIS_DIM // 2
GLA_HEADS = 4
GLA_DK = 128
GLA_DV = 256
GLA_HK = GLA_HEADS * GLA_DK
GLA_HV = GLA_HEADS * GLA_DV
GLA_GATE_RANK = 16
GLA_GATE_TAU = 16.0
GLA_CHUNK = 64
D_FF = 2816

BF16 = jnp.bfloat16
F32 = jnp.float32

MOD_ROWS = 16
ADA_TN = 2304
TOKEN_TILE = 512
ATTN_TQ = 512
ATTN_KEY_CHUNK = 256
QK_PRESCALE = HEAD_DIM ** -0.5 * 1.4426950408889634
SCAN_TOKENS = 256
VMEM_LIMIT_BYTES = 56 * 1024 * 1024


def _cparams(*sem):
    return pltpu.CompilerParams(dimension_semantics=sem, vmem_limit_bytes=VMEM_LIMIT_BYTES)


def _resident(shape):
    nd = len(shape)
    return pl.BlockSpec(shape, lambda *_: (0,) * nd, pipeline_mode=pl.Buffered(1))


def _resident_slice(stacked_shape, lead):
    tail = tuple(stacked_shape[len(lead):])
    index = tuple(lead) + (0,) * len(tail)
    return pl.BlockSpec((None,) * len(lead) + tail, lambda *_: index, pipeline_mode=pl.Buffered(1))


def _dot(a, b):
    return jnp.dot(a, b, preferred_element_type=F32)


def _dot_nt(a, b):
    return lax.dot_general(a, b, (((1,), (1,)), ((), ())), preferred_element_type=F32)


def _dot_tn(a, b):
    return lax.dot_general(a, b, (((0,), (0,)), ((), ())), preferred_element_type=F32)


def _rms(x, g):
    return x * lax.rsqrt(jnp.mean(x * x, axis=-1, keepdims=True) + EPS) * g


def _silu(x):
    return x * jax.nn.sigmoid(x)


def _log_sigmoid(x):
    return jnp.minimum(x, 0.0) - jnp.log1p(jnp.exp(-jnp.abs(x)))


def _modulated(h, mod_ref, gpre_ref, j):
    shift = mod_ref[3 * j:3 * j + 1, :]
    scale = mod_ref[3 * j + 1:3 * j + 2, :]
    return (_rms(h, gpre_ref[j:j + 1, :]) * (1.0 + scale) + shift).astype(BF16)


def _ada_kernel(c_ref, w_ref, b_ref, o_ref):
    s = _silu(c_ref[...]).astype(BF16)
    o_ref[...] = _dot(s, w_ref[...].astype(BF16)) + b_ref[...]


def _ada(cc, ada_w, ada_b):
    n = N_MOD * D_MODEL
    return pl.pallas_call(
        _ada_kernel,
        grid=(DEPTH, n // ADA_TN),
        in_specs=[
            _resident((MOD_ROWS, D_MODEL)),
            pl.BlockSpec((None, D_MODEL, ADA_TN), lambda l, t: (l, 0, t)),
            pl.BlockSpec((None, 1, ADA_TN), lambda l, t: (l, 0, t)),
        ],
        out_specs=pl.BlockSpec((None, MOD_ROWS, ADA_TN), lambda l, t: (l, 0, t)),
        out_shape=jax.ShapeDtypeStruct((DEPTH, MOD_ROWS, n), F32),
        compiler_params=_cparams("parallel", "parallel"),
        name="ada_mod",
    )(cc, ada_w, ada_b.reshape(DEPTH, 1, n))


def _ffn_kernel(j, h_ref, mod_ref, gpre_ref, gpost_ref, win_ref, wout_ref, o_ref):
    h = h_ref[...]
    u = _modulated(h, mod_ref, gpre_ref, j)
    gu = _dot(u, win_ref[...])
    a = (_silu(gu[:, :D_FF]) * gu[:, D_FF:]).astype(BF16)
    y = _dot(a, wout_ref[...])
    gate = mod_ref[3 * j + 2:3 * j + 3, :]
    o_ref[...] = h + 0.5 * gate * _rms(y, gpost_ref[j:j + 1, :])


def _tok_spec(width, tm=TOKEN_TILE):
    return pl.BlockSpec((None, tm, width), lambda b, t: (b, t, 0))


def _mod_spec():
    return pl.BlockSpec((None, N_MOD, D_MODEL), lambda b, t: (b, 0, 0))


def _ffn(h, mod, gpre, gpost, w_in, w_out, layer, j):
    nb, s, _ = h.shape
    which = (layer, j // 2)
    return pl.pallas_call(
        functools.partial(_ffn_kernel, j),
        grid=(nb, s // TOKEN_TILE),
        in_specs=[
            _tok_spec(D_MODEL), _mod_spec(),
            _resident((3, D_MODEL)), _resident((3, D_MODEL)),
            _resident_slice(w_in.shape, which), _resident_slice(w_out.shape, which),
        ],
        out_specs=_tok_spec(D_MODEL),
        out_shape=jax.ShapeDtypeStruct(h.shape, F32),
        compiler_params=_cparams("parallel", "parallel"),
        name=f"ffn_half{j}",
    )(h, mod, gpre, gpost, w_in, w_out)


def _rope_partner(x):
    lane = lax.broadcasted_iota(jnp.int32, x.shape, 1)
    return jnp.where((lane & ROPE_HALF) == 0,
                     pltpu.roll(x, HEAD_DIM - ROPE_HALF, 1), pltpu.roll(x, ROPE_HALF, 1))


def _attn_prep_kernel(use_rope, h_ref, mod_ref, gpre_ref, w_ref, qg_ref, kg_ref, *refs):
    if use_rope:
        cos_ref, sin_ref, q_ref, k_ref, v_ref = refs
    else:
        q_ref, k_ref, v_ref = refs
    u = _modulated(h_ref[...], mod_ref, gpre_ref, 1)
    qkv = _dot(u, w_ref[...])

    def head(col, gain_ref, post_scale):
        x = _rms(qkv[:, col:col + HEAD_DIM], gain_ref[...])
        if use_rope:
            x = x * cos_ref[...] + _rope_partner(x) * sin_ref[...]
        if post_scale is not None:
            x = x * post_scale
        return x.astype(BF16)

    for hd in range(N_HEADS):
        q_ref[:, hd * HEAD_DIM:(hd + 1) * HEAD_DIM] = head(hd * HEAD_DIM, qg_ref, QK_PRESCALE)
    k0 = N_HEADS * HEAD_DIM
    for hd in range(N_KV_HEADS):
        k_ref[:, hd * HEAD_DIM:(hd + 1) * HEAD_DIM] = head(k0 + hd * HEAD_DIM, kg_ref, None)
    v0 = k0 + N_KV_HEADS * HEAD_DIM
    ones = jnp.ones((qkv.shape[0], HEAD_DIM), BF16)
    for hd in range(N_KV_HEADS):
        v_ref[:, 2 * hd * HEAD_DIM:(2 * hd + 1) * HEAD_DIM] = (
            qkv[:, v0 + hd * HEAD_DIM:v0 + (hd + 1) * HEAD_DIM].astype(BF16))
        v_ref[:, (2 * hd + 1) * HEAD_DIM:(2 * hd + 2) * HEAD_DIM] = ones


def _attn_prep(h, mod, gpre, w_qkv, m, q_gain, k_gain, rope):
    nb, s, _ = h.shape
    kvw = N_KV_HEADS * HEAD_DIM
    in_specs = [
        _tok_spec(D_MODEL), _mod_spec(), _resident((3, D_MODEL)),
        _resident_slice(w_qkv.shape, (m,)),
        _resident((1, HEAD_DIM)), _resident((1, HEAD_DIM)),
    ]
    args = [h, mod, gpre, w_qkv, q_gain, k_gain]
    if rope is not None:
        in_specs += [pl.BlockSpec((TOKEN_TILE, HEAD_DIM), lambda b, t: (t, 0))] * 2
        args += list(rope)
    return pl.pallas_call(
        functools.partial(_attn_prep_kernel, rope is not None),
        grid=(nb, s // TOKEN_TILE),
        in_specs=in_specs,
        out_specs=[_tok_spec(D_MODEL), _tok_spec(kvw), _tok_spec(2 * kvw)],
        out_shape=[jax.ShapeDtypeStruct((nb, s, D_MODEL), BF16),
                   jax.ShapeDtypeStruct((nb, s, kvw), BF16),
                   jax.ShapeDtypeStruct((nb, s, 2 * kvw), BF16)],
        compiler_params=_cparams("parallel", "parallel"),
        name="attn_prep_rope" if rope is not None else "attn_prep_ctx",
    )(*args)


def _attention_kernel(chunks, q_ref, *refs):
    kv_refs, o_ref = refs[:-1], refs[-1]
    q = q_ref[...]
    qs = jnp.concatenate([q[:, g * HEAD_DIM:(g + 1) * HEAD_DIM] for g in range(Q_PER_KV)], axis=0)
    m = out = None
    for seg, lo, n in chunks:
        s = _dot_nt(qs, kv_refs[2 * seg][lo:lo + n, :])
        v = kv_refs[2 * seg + 1][lo:lo + n, :]
        row_max = jnp.max(s, axis=-1, keepdims=True)
        if m is None:
            m = row_max
            out = _dot(jnp.exp2(s - m).astype(BF16), v)
        else:
            m_new = jnp.maximum(m, row_max)
            out = jnp.exp2(m - m_new) * out + _dot(jnp.exp2(s - m_new).astype(BF16), v)
            m = m_new
    out = out[:, :HEAD_DIM] / out[:, HEAD_DIM:]
    tq = q.shape[0]
    for g in range(Q_PER_KV):
        o_ref[:, g * HEAD_DIM:(g + 1) * HEAD_DIM] = out[g * tq:(g + 1) * tq, :].astype(BF16)


def _attention(q, segments, tq):
    b, s, _ = q.shape
    gw = Q_PER_KV * HEAD_DIM
    in_specs = [pl.BlockSpec((None, tq, gw), lambda bi, kh, t: (bi, t, kh))]
    args = [q]
    chunks = []
    for seg, (k, v) in enumerate(segments):
        n_keys = k.shape[1]
        in_specs += [pl.BlockSpec((None, n_keys, HEAD_DIM), lambda bi, kh, t: (bi, 0, kh)),
                     pl.BlockSpec((None, n_keys, 2 * HEAD_DIM), lambda bi, kh, t: (bi, 0, kh))]
        args += [k, v]
        step = min(ATTN_KEY_CHUNK, n_keys)
        chunks += [(seg, lo, step) for lo in range(0, n_keys, step)]
    return pl.pallas_call(
        functools.partial(_attention_kernel, tuple(chunks)),
        grid=(b, N_KV_HEADS, s // tq),
        in_specs=in_specs,
        out_specs=pl.BlockSpec((None, tq, gw), lambda bi, kh, t: (bi, t, kh)),
        out_shape=jax.ShapeDtypeStruct(q.shape, BF16),
        compiler_params=_cparams("parallel", "parallel", "parallel"),
        name=f"attention_{len(segments)}seg",
    )(*args)


def _mix_residual(z, w_ref, h_ref, mod_ref, gpost_ref, out_ref):
    y = _dot(z, w_ref[...])
    out_ref[...] = h_ref[...] + mod_ref[5:6, :] * _rms(y, gpost_ref[1:2, :])


def _attn_out_kernel(o_ref, w_ref, h_ref, mod_ref, gpost_ref, out_ref):
    _mix_residual(o_ref[...], w_ref, h_ref, mod_ref, gpost_ref, out_ref)


def _attn_out(o, w_o, m, h, mod, gpost):
    nb, s, _ = h.shape
    return pl.pallas_call(
        _attn_out_kernel,
        grid=(nb, s // TOKEN_TILE),
        in_specs=[_tok_spec(D_MODEL), _resident_slice(w_o.shape, (m,)), _tok_spec(D_MODEL),
                  _mod_spec(), _resident((3, D_MODEL))],
        out_specs=_tok_spec(D_MODEL),
        out_shape=jax.ShapeDtypeStruct(h.shape, F32),
        compiler_params=_cparams("parallel", "parallel"),
        name="attn_out",
    )(o, w_o, h, mod, gpost)


def _gla_out_kernel(of_ref, ob_ref, r_ref, og_ref, w_ref, h_ref, mod_ref, gpost_ref, out_ref):
    o = of_ref[...].astype(F32) + ob_ref[...].astype(F32)
    normed = jnp.concatenate(
        [_rms(o[:, hd * GLA_DV:(hd + 1) * GLA_DV], og_ref[:, hd * GLA_DV:(hd + 1) * GLA_DV])
         for hd in range(GLA_HEADS)], axis=-1)
    z = (normed * _silu(r_ref[...].astype(F32))).astype(BF16)
    _mix_residual(z, w_ref, h_ref, mod_ref, gpost_ref, out_ref)


def _gla_out(o_f, o_b, r, o_gain, w_o, m, h, mod, gpost):
    nb, s, _ = h.shape
    return pl.pallas_call(
        _gla_out_kernel,
        grid=(nb, s // TOKEN_TILE),
        in_specs=[_tok_spec(GLA_HV), _tok_spec(GLA_HV), _tok_spec(GLA_HV),
                  _resident((1, GLA_HV)), _resident_slice(w_o.shape, (m,)), _tok_spec(D_MODEL),
                  _mod_spec(), _resident((3, D_MODEL))],
        out_specs=_tok_spec(D_MODEL),
        out_shape=jax.ShapeDtypeStruct(h.shape, F32),
        compiler_params=_cparams("parallel", "parallel"),
        name="gla_out",
    )(o_f, o_b, r, o_gain, w_o, h, mod, gpost)


def _chunk_cumsum(g, reverse):
    rows = g.shape[0]
    pos = lax.broadcasted_iota(jnp.int32, g.shape, 0) % GLA_CHUNK
    x = g
    step = 1
    while step < GLA_CHUNK:
        if reverse:
            x = x + jnp.where(pos < GLA_CHUNK - step, pltpu.roll(x, rows - step, 0), 0.0)
        else:
            x = x + jnp.where(pos >= step, pltpu.roll(x, step, 0), 0.0)
        step *= 2
    return x


def _gla_prep_kernel(h_ref, mod_ref, gpre_ref, win_ref, wa1_ref, wa2_ref, ba_ref,
                     qdf_ref, kif_ref, kef_ref, qdb_ref, kib_ref, keb_ref, v_ref, r_ref, dec_ref,
                     cum_ref):
    u = _modulated(h_ref[...], mod_ref, gpre_ref, 1)
    proj = _dot(u, win_ref[...])
    q = proj[:, :GLA_HK] * (GLA_DK ** -0.5)
    k = proj[:, GLA_HK:2 * GLA_HK]
    v_ref[...] = proj[:, 2 * GLA_HK:2 * GLA_HK + GLA_HV].astype(BF16)
    r_ref[...] = proj[:, 2 * GLA_HK + GLA_HV:].astype(BF16)
    low = _dot(u, wa1_ref[...]).astype(BF16)
    gates = _log_sigmoid(_dot(low, wa2_ref[...]) + ba_ref[...]) / GLA_GATE_TAU
    tm = q.shape[0]
    n_chunks = tm // GLA_CHUNK
    outs = ((qdf_ref, kif_ref, kef_ref), (qdb_ref, kib_ref, keb_ref))
    for d, (qd_ref, ki_ref, ke_ref) in enumerate(outs):
        cum = _chunk_cumsum(gates[:, d * GLA_HK:(d + 1) * GLA_HK], reverse=(d == 1))
        qd_ref[...] = (q * jnp.exp(cum)).astype(BF16)
        ki_ref[...] = (k * jnp.exp(-cum)).astype(BF16)
        edge = GLA_CHUNK - 1 if d == 0 else 0
        totals = []
        for hd in range(GLA_HEADS):
            cum_ref[hd] = cum[:, hd * GLA_DK:(hd + 1) * GLA_DK]
            totals.append(cum_ref[hd, pl.ds(edge, n_chunks, stride=GLA_CHUNK), :])
        total = jnp.concatenate(totals, axis=-1)
        dec_ref[:, d * GLA_HK:(d + 1) * GLA_HK] = jnp.exp(total)
        spread = jnp.broadcast_to(total[:, None, :], (n_chunks, GLA_CHUNK, GLA_HK)).reshape(tm, GLA_HK)
        ke_ref[...] = (k * jnp.exp(spread - cum)).astype(BF16)


def _gla_prep(h, mod, gpre, w_in, m, wa1, wa2, ba):
    nb, s, _ = h.shape
    nck = TOKEN_TILE // GLA_CHUNK
    half = jax.ShapeDtypeStruct((nb, s, GLA_HK), BF16)
    full = jax.ShapeDtypeStruct((nb, s, GLA_HV), BF16)
    return pl.pallas_call(
        _gla_prep_kernel,
        grid=(nb, s // TOKEN_TILE),
        in_specs=[_tok_spec(D_MODEL), _mod_spec(), _resident((3, D_MODEL)),
                  _resident_slice(w_in.shape, (m,)), _resident(wa1.shape), _resident(wa2.shape),
                  _resident(ba.shape)],
        out_specs=[_tok_spec(GLA_HK)] * 6 + [_tok_spec(GLA_HV)] * 2
        + [pl.BlockSpec((None, nck, 2 * GLA_HK), lambda b, t: (b, t, 0))],
        out_shape=[half] * 6 + [full] * 2
        + [jax.ShapeDtypeStruct((nb, s // GLA_CHUNK, 2 * GLA_HK), F32)],
        scratch_shapes=[pltpu.VMEM((GLA_HEADS, TOKEN_TILE, GLA_DK), F32)],
        compiler_params=_cparams("parallel", "parallel"),
        name="gla_prep",
    )(h, mod, gpre, w_in, wa1, wa2, ba)


def _gla_scan_kernel(n_blocks, qdf_ref, kif_ref, kef_ref, vf_ref, qdb_ref, kib_ref, keb_ref, vb_ref,
                     dec_ref, s0_ref, of_ref, ob_ref, st_ref):
    t = pl.program_id(1)

    @pl.when(t == 0)
    def _():
        st_ref[...] = s0_ref[...]

    n_chunks = qdf_ref.shape[0] // GLA_CHUNK
    row = lax.broadcasted_iota(jnp.int32, (GLA_CHUNK, GLA_CHUNK), 0)
    col = lax.broadcasted_iota(jnp.int32, (GLA_CHUNK, GLA_CHUNK), 1)
    dirs = ((qdf_ref, kif_ref, kef_ref, vf_ref, of_ref, col <= row),
            (qdb_ref, kib_ref, keb_ref, vb_ref, ob_ref, col >= row))
    for d, (qd_ref, ki_ref, ke_ref, v_ref, o_ref, mask) in enumerate(dirs):
        block = t if d == 0 else n_blocks - 1 - t
        for step in range(n_chunks):
            c = step if d == 0 else n_chunks - 1 - step
            rows = slice(c * GLA_CHUNK, (c + 1) * GLA_CHUNK)
            decay = dec_ref[pl.ds(block * n_chunks + c, 1), :]
            for hd in range(GLA_HEADS):
                kl = slice(hd * GLA_DK, (hd + 1) * GLA_DK)
                vl = slice(hd * GLA_DV, (hd + 1) * GLA_DV)
                qd, vv = qd_ref[rows, kl], v_ref[rows, vl]
                state = st_ref[d * GLA_HEADS + hd]
                scores = jnp.where(mask, _dot_nt(qd, ki_ref[rows, kl]), 0.0)
                o_ref[rows, vl] = (
                    _dot(scores.astype(BF16), vv) + _dot_nt(qd, state.astype(BF16))).astype(BF16)
                st_ref[d * GLA_HEADS + hd] = (
                    decay[:, d * GLA_HK + hd * GLA_DK:d * GLA_HK + (hd + 1) * GLA_DK] * state
                    + _dot_tn(vv, ke_ref[rows, kl]))


def _gla_scan(qdf, kif, kef, qdb, kib, keb, v, dec, state0):
    b, s, _ = v.shape
    n_blocks = s // SCAN_TOKENS
    fwd = lambda bi, t: (bi, t, 0)
    bwd = lambda bi, t: (bi, n_blocks - 1 - t, 0)
    hk = lambda m: pl.BlockSpec((None, SCAN_TOKENS, GLA_HK), m)
    hv = lambda m: pl.BlockSpec((None, SCAN_TOKENS, GLA_HV), m)
    st_spec = pl.BlockSpec((None, 2 * GLA_HEADS, GLA_DV, GLA_DK), lambda bi, t: (bi, 0, 0, 0))
    return pl.pallas_call(
        functools.partial(_gla_scan_kernel, n_blocks),
        grid=(b, n_blocks),
        in_specs=[hk(fwd), hk(fwd), hk(fwd), hv(fwd), hk(bwd), hk(bwd), hk(bwd), hv(bwd),
                  pl.BlockSpec((None, s // GLA_CHUNK, 2 * GLA_HK), lambda bi, t: (bi, 0, 0)),
                  st_spec],
        out_specs=[hv(fwd), hv(bwd), st_spec],
        out_shape=[jax.ShapeDtypeStruct((b, s, GLA_HV), BF16)] * 2
        + [jax.ShapeDtypeStruct(state0.shape, F32)],
        compiler_params=_cparams("parallel", "arbitrary"),
        name="gla_scan",
    )(qdf, kif, kef, v, qdb, kib, keb, v, dec, state0)


def _rope_tables(n_tok):
    pos = jnp.arange(n_tok, dtype=jnp.int32)
    inv = ROPE_THETA ** (-jnp.arange(0, ROPE_AXIS_DIM, 2, dtype=F32) / ROPE_AXIS_DIM)
    ang_r = (pos // GRID_W).astype(F32)[:, None] * inv[None, :]
    ang_c = (pos % GRID_W).astype(F32)[:, None] * inv[None, :]
    cos = jnp.concatenate([jnp.cos(ang_r)] * 2 + [jnp.cos(ang_c)] * 2, axis=-1)
    sin = jnp.concatenate([-jnp.sin(ang_r), jnp.sin(ang_r), -jnp.sin(ang_c), jnp.sin(ang_c)], axis=-1)
    return cos, sin


def kernel(x, c, ctx, c_ctx, ada_w, ada_b, norm_pre, norm_post, ffn_w_in, ffn_w_out, attn_w_qkv, attn_q_gain, attn_k_gain, attn_w_o, gla_w_in, gla_wa1, gla_wa2, gla_ba, gla_o_gain, gla_w_o):
    b, n_tok, _ = x.shape
    n_ctx = ctx.shape[1]
    assert b + 1 <= MOD_ROWS and n_tok % TOKEN_TILE == 0 and (b * n_ctx) % TOKEN_TILE == 0

    cc = jnp.concatenate([c, c_ctx[None, :], jnp.zeros((MOD_ROWS - b - 1, D_MODEL), F32)], axis=0)
    mods = _ada(cc, ada_w, ada_b).reshape(DEPTH, MOD_ROWS, N_MOD, D_MODEL)
    rope = _rope_tables(n_tok)
    w_in, w_out = ffn_w_in.astype(BF16), ffn_w_out.astype(BF16)
    w_qkv, w_attn_o = attn_w_qkv.astype(BF16), attn_w_o.astype(BF16)
    w_gla_in, w_gla_o = gla_w_in.astype(BF16), gla_w_o.astype(BF16)

    def per_batch(a):
        return a.reshape(b, n_ctx, a.shape[-1])

    hx = x
    hc = ctx.reshape(1, b * n_ctx, D_MODEL)
    for i in range(DEPTH):
        last = i == DEPTH - 1
        mod_x, mod_c = mods[i, :b], mods[i, b:b + 1]
        gpre, gpost = norm_pre[i], norm_post[i]

        hx = _ffn(hx, mod_x, gpre, gpost, w_in, w_out, i, 0)
        hc = _ffn(hc, mod_c, gpre, gpost, w_in, w_out, i, 0)

        m = i // 2
        if i % 2 == 0:
            qg, kg = attn_q_gain[m][None, :], attn_k_gain[m][None, :]
            qx, kx, vx = _attn_prep(hx, mod_x, gpre, w_qkv, m, qg, kg, rope)
            qc, kc, vc = (per_batch(a) for a in _attn_prep(hc, mod_c, gpre, w_qkv, m, qg, kg, None))
            ox = _attention(qx, [(kc, vc), (kx, vx)], ATTN_TQ)
            hx = _attn_out(ox, w_attn_o, m, hx, mod_x, gpost)
            if not last:
                oc = _attention(qc, [(kc, vc)], n_ctx)
                hc = _attn_out(oc.reshape(hc.shape), w_attn_o, m, hc, mod_c, gpost)
        else:
            wa1 = jnp.concatenate([gla_wa1[m, 0], gla_wa1[m, 1]], axis=-1).astype(BF16)
            zeros = jnp.zeros((GLA_GATE_RANK, GLA_HK), F32)
            wa2 = jnp.concatenate([jnp.concatenate([gla_wa2[m, 0], zeros], axis=-1),
                                   jnp.concatenate([zeros, gla_wa2[m, 1]], axis=-1)], axis=0).astype(BF16)
            ba = gla_ba[m].reshape(1, 2 * GLA_HK)
            o_gain = gla_o_gain[m][None, :]
            px = _gla_prep(hx, mod_x, gpre, w_gla_in, m, wa1, wa2, ba)
            pc = _gla_prep(hc, mod_c, gpre, w_gla_in, m, wa1, wa2, ba)
            pc = [per_batch(a) for a in pc[:8]] + [pc[8].reshape(b, n_ctx // GLA_CHUNK, 2 * GLA_HK)]
            zero_state = jnp.zeros((b, 2 * GLA_HEADS, GLA_DV, GLA_DK), F32)
            oc_f, oc_b, ctx_state = _gla_scan(*pc[:7], pc[8], zero_state)
            ox_f, ox_b, _ = _gla_scan(*px[:7], px[8], ctx_state)
            hx = _gla_out(ox_f, ox_b, px[7], o_gain, w_gla_o, m, hx, mod_x, gpost)
            if not last:
                hc = _gla_out(oc_f.reshape(1, b * n_ctx, GLA_HV), oc_b.reshape(1, b * n_ctx, GLA_HV),
                              pc[7].reshape(1, b * n_ctx, GLA_HV), o_gain, w_gla_o, m, hc, mod_c, gpost)

        hx = _ffn(hx, mod_x, gpre, gpost, w_in, w_out, i, 2)
        if not last:
            hc = _ffn(hc, mod_c, gpre, gpost, w_in, w_out, i, 2)
    return hx
```

```python
import functools

import jax
import jax.numpy as jnp
from jax import lax
from jax.experimental import pallas as pl
from jax.experimental.pallas import tpu as pltpu

D_MODEL = 1024
DEPTH = 4
GRID_W = 64
N_MOD = 9
EPS = 1e-6
HEAD_DIM = 128
N_HEADS = 8
N_KV_HEADS = 2
Q_PER_KV = N_HEADS // N_KV_HEADS
ROPE_THETA = 10000.0
ROPE_AXIS_DIM = HEAD_DIM // 2
ROPE_HALF = ROPE_AXIS_DIM // 2
GLA_HEADS = 4
GLA_DK = 128
GLA_DV = 256
GLA_HK = GLA_HEADS * GLA_DK
GLA_HV = GLA_HEADS * GLA_DV
GLA_GATE_RANK = 16
GLA_GATE_TAU = 16.0
GLA_CHUNK = 64
D_FF = 2816

BF16 = jnp.bfloat16
F32 = jnp.float32

MOD_ROWS = 16
ADA_TN = 2304
TOKEN_TILE = 512
FFN_TILE = 1024
SUB_ROWS = 256
ATTN_TQ = 512
ATTN_KEY_CHUNK = 256
QK_PRESCALE = HEAD_DIM ** -0.5 * 1.4426950408889634
SCAN_TOKENS = 256
VMEM_LIMIT_BYTES = 56 * 1024 * 1024


def _cparams(*sem):
    return pltpu.CompilerParams(dimension_semantics=sem, vmem_limit_bytes=VMEM_LIMIT_BYTES)


def _resident(shape):
    nd = len(shape)
    return pl.BlockSpec(shape, lambda *_: (0,) * nd, pipeline_mode=pl.Buffered(1))


def _resident_slice(stacked_shape, lead):
    tail = tuple(stacked_shape[len(lead):])
    index = tuple(lead) + (0,) * len(tail)
    return pl.BlockSpec((None,) * len(lead) + tail, lambda *_: index, pipeline_mode=pl.Buffered(1))


def _dot(a, b):
    return jnp.dot(a, b, preferred_element_type=F32)


def _dot_nt(a, b):
    return lax.dot_general(a, b, (((1,), (1,)), ((), ())), preferred_element_type=F32)


def _dot_tn(a, b):
    return lax.dot_general(a, b, (((0,), (0,)), ((), ())), preferred_element_type=F32)


def _rms(x, g):
    return x * lax.rsqrt(jnp.mean(x * x, axis=-1, keepdims=True) + EPS) * g


def _silu(x):
    return x * jax.nn.sigmoid(x)


def _log_sigmoid(x):
    return jnp.minimum(x, 0.0) - jnp.log1p(jnp.exp(-jnp.abs(x)))


def _modulated(h, mod_ref, gpre_ref, j):
    shift = mod_ref[3 * j:3 * j + 1, :]
    scale = mod_ref[3 * j + 1:3 * j + 2, :]
    return (_rms(h, gpre_ref[j:j + 1, :]) * (1.0 + scale) + shift).astype(BF16)


def _ada_kernel(c_ref, w_ref, b_ref, o_ref):
    s = _silu(c_ref[...]).astype(BF16)
    o_ref[...] = _dot(s, w_ref[...].astype(BF16)) + b_ref[...]


def _ada(cc, ada_w, ada_b):
    n = N_MOD * D_MODEL
    return pl.pallas_call(
        _ada_kernel,
        grid=(DEPTH, n // ADA_TN),
        in_specs=[
            _resident((MOD_ROWS, D_MODEL)),
            pl.BlockSpec((None, D_MODEL, ADA_TN), lambda l, t: (l, 0, t)),
            pl.BlockSpec((None, 1, ADA_TN), lambda l, t: (l, 0, t)),
        ],
        out_specs=pl.BlockSpec((None, MOD_ROWS, ADA_TN), lambda l, t: (l, 0, t)),
        out_shape=jax.ShapeDtypeStruct((DEPTH, MOD_ROWS, n), F32),
        compiler_params=_cparams("parallel", "parallel"),
        name="ada_mod",
    )(cc, ada_w, ada_b.reshape(DEPTH, 1, n))


def _sub_blocks(n_rows):
    return [slice(r0, r0 + SUB_ROWS) for r0 in range(0, n_rows, SUB_ROWS)]


def _swiglu_half(h, j, mod_ref, gpre_ref, gpost_ref, win_ref, wout_ref):
    u = _modulated(h, mod_ref, gpre_ref, j)
    gu = _dot(u, win_ref[...])
    a = (_silu(gu[:, :D_FF]) * gu[:, D_FF:]).astype(BF16)
    y = _dot(a, wout_ref[...])
    return h + 0.5 * mod_ref[3 * j + 2:3 * j + 3, :] * _rms(y, gpost_ref[j:j + 1, :])


def _ffn_kernel(h_ref, mod_ref, gpre_ref, gpost_ref, win_ref, wout_ref, o_ref):
    for rows in _sub_blocks(h_ref.shape[0]):
        o_ref[rows, :] = _swiglu_half(h_ref[rows, :], 0, mod_ref, gpre_ref, gpost_ref, win_ref, wout_ref)


def _tok_spec(width, tm=TOKEN_TILE):
    return pl.BlockSpec((None, tm, width), lambda b, t: (b, t, 0))


def _mod_spec():
    return pl.BlockSpec((None, N_MOD, D_MODEL), lambda b, t: (b, 0, 0))


def _ffn_pre(h, mod, gpre, gpost, w_in, w_out, layer):
    nb, s, _ = h.shape
    return pl.pallas_call(
        _ffn_kernel,
        grid=(nb, s // FFN_TILE),
        in_specs=[
            _tok_spec(D_MODEL, FFN_TILE), _mod_spec(),
            _resident((3, D_MODEL)), _resident((3, D_MODEL)),
            _resident_slice(w_in.shape, (layer, 0)), _resident_slice(w_out.shape, (layer, 0)),
        ],
        out_specs=_tok_spec(D_MODEL, FFN_TILE),
        out_shape=jax.ShapeDtypeStruct(h.shape, F32),
        compiler_params=_cparams("parallel", "parallel"),
        name="ffn_pre",
    )(h, mod, gpre, gpost, w_in, w_out)


def _rope_partner(x):
    lane = lax.broadcasted_iota(jnp.int32, x.shape, 1)
    return jnp.where((lane & ROPE_HALF) == 0,
                     pltpu.roll(x, HEAD_DIM - ROPE_HALF, 1), pltpu.roll(x, ROPE_HALF, 1))


def _attn_prep_kernel(use_rope, h_ref, mod_ref, gpre_ref, w_ref, qg_ref, kg_ref, *refs):
    if use_rope:
        cos_ref, sin_ref, q_ref, k_ref, v_ref = refs
    else:
        q_ref, k_ref, v_ref = refs
    k0 = N_HEADS * HEAD_DIM
    v0 = k0 + N_KV_HEADS * HEAD_DIM
    ones = jnp.ones((SUB_ROWS, HEAD_DIM), BF16)
    for rows in _sub_blocks(h_ref.shape[0]):
        u = _modulated(h_ref[rows, :], mod_ref, gpre_ref, 1)
        qkv = _dot(u, w_ref[...])

        def head(col, gain_ref, post_scale):
            x = _rms(qkv[:, col:col + HEAD_DIM], gain_ref[...])
            if use_rope:
                x = x * cos_ref[rows, :] + _rope_partner(x) * sin_ref[rows, :]
            if post_scale is not None:
                x = x * post_scale
            return x.astype(BF16)

        for hd in range(N_HEADS):
            q_ref[rows, hd * HEAD_DIM:(hd + 1) * HEAD_DIM] = head(hd * HEAD_DIM, qg_ref, QK_PRESCALE)
        for hd in range(N_KV_HEADS):
            k_ref[rows, hd * HEAD_DIM:(hd + 1) * HEAD_DIM] = head(k0 + hd * HEAD_DIM, kg_ref, None)
        for hd in range(N_KV_HEADS):
            v_ref[rows, 2 * hd * HEAD_DIM:(2 * hd + 1) * HEAD_DIM] = (
                qkv[:, v0 + hd * HEAD_DIM:v0 + (hd + 1) * HEAD_DIM].astype(BF16))
            v_ref[rows, (2 * hd + 1) * HEAD_DIM:(2 * hd + 2) * HEAD_DIM] = ones


def _attn_prep(h, mod, gpre, w_qkv, m, q_gain, k_gain, rope):
    nb, s, _ = h.shape
    kvw = N_KV_HEADS * HEAD_DIM
    in_specs = [
        _tok_spec(D_MODEL), _mod_spec(), _resident((3, D_MODEL)),
        _resident_slice(w_qkv.shape, (m,)),
        _resident((1, HEAD_DIM)), _resident((1, HEAD_DIM)),
    ]
    args = [h, mod, gpre, w_qkv, q_gain, k_gain]
    if rope is not None:
        in_specs += [pl.BlockSpec((TOKEN_TILE, HEAD_DIM), lambda b, t: (t, 0))] * 2
        args += list(rope)
    return pl.pallas_call(
        functools.partial(_attn_prep_kernel, rope is not None),
        grid=(nb, s // TOKEN_TILE),
        in_specs=in_specs,
        out_specs=[_tok_spec(D_MODEL), _tok_spec(kvw), _tok_spec(2 * kvw)],
        out_shape=[jax.ShapeDtypeStruct((nb, s, D_MODEL), BF16),
                   jax.ShapeDtypeStruct((nb, s, kvw), BF16),
                   jax.ShapeDtypeStruct((nb, s, 2 * kvw), BF16)],
        compiler_params=_cparams("parallel", "parallel"),
        name="attn_prep_rope" if rope is not None else "attn_prep_ctx",
    )(*args)


def _attention_kernel(chunks, q_ref, *refs):
    kv_refs, o_ref = refs[:-1], refs[-1]
    q = q_ref[...]
    qs = jnp.concatenate([q[:, g * HEAD_DIM:(g + 1) * HEAD_DIM] for g in range(Q_PER_KV)], axis=0)
    m = out = None
    for seg, lo, n in chunks:
        s = _dot_nt(qs, kv_refs[2 * seg][lo:lo + n, :])
        v = kv_refs[2 * seg + 1][lo:lo + n, :]
        row_max = jnp.max(s, axis=-1, keepdims=True)
        if m is None:
            m = row_max
            out = _dot(jnp.exp2(s - m).astype(BF16), v)
        else:
            m_new = jnp.maximum(m, row_max)
            out = jnp.exp2(m - m_new) * out + _dot(jnp.exp2(s - m_new).astype(BF16), v)
            m = m_new
    out = out[:, :HEAD_DIM] / out[:, HEAD_DIM:]
    tq = q.shape[0]
    for g in range(Q_PER_KV):
        o_ref[:, g * HEAD_DIM:(g + 1) * HEAD_DIM] = out[g * tq:(g + 1) * tq, :].astype(BF16)


def _attention(q, segments, tq):
    b, s, _ = q.shape
    gw = Q_PER_KV * HEAD_DIM
    in_specs = [pl.BlockSpec((None, tq, gw), lambda bi, kh, t: (bi, t, kh))]
    args = [q]
    chunks = []
    for seg, (k, v) in enumerate(segments):
        n_keys = k.shape[1]
        in_specs += [pl.BlockSpec((None, n_keys, HEAD_DIM), lambda bi, kh, t: (bi, 0, kh)),
                     pl.BlockSpec((None, n_keys, 2 * HEAD_DIM), lambda bi, kh, t: (bi, 0, kh))]
        args += [k, v]
        step = min(ATTN_KEY_CHUNK, n_keys)
        chunks += [(seg, lo, step) for lo in range(0, n_keys, step)]
    return pl.pallas_call(
        functools.partial(_attention_kernel, tuple(chunks)),
        grid=(b, N_KV_HEADS, s // tq),
        in_specs=in_specs,
        out_specs=pl.BlockSpec((None, tq, gw), lambda bi, kh, t: (bi, t, kh)),
        out_shape=jax.ShapeDtypeStruct(q.shape, BF16),
        compiler_params=_cparams("parallel", "parallel", "parallel"),
        name=f"attention_{len(segments)}seg",
    )(*args)


def _mix_residual(z, h, wo_ref, mod_ref, gpost_ref):
    return h + mod_ref[5:6, :] * _rms(_dot(z, wo_ref[...]), gpost_ref[1:2, :])


def _attn_out_kernel(o_ref, wo_ref, h_ref, mod_ref, gpre_ref, gpost_ref, win_ref, wout_ref, out_ref):
    for rows in _sub_blocks(h_ref.shape[0]):
        h = _mix_residual(o_ref[rows, :], h_ref[rows, :], wo_ref, mod_ref, gpost_ref)
        out_ref[rows, :] = _swiglu_half(h, 2, mod_ref, gpre_ref, gpost_ref, win_ref, wout_ref)


def _post_specs(w_o, m, w_in, w_out, layer):
    return [_resident_slice(w_o.shape, (m,)), _tok_spec(D_MODEL), _mod_spec(),
            _resident((3, D_MODEL)), _resident((3, D_MODEL)),
            _resident_slice(w_in.shape, (layer, 1)), _resident_slice(w_out.shape, (layer, 1))]


def _attn_out(o, w_o, m, h, mod, gpre, gpost, w_in, w_out, layer):
    nb, s, _ = h.shape
    return pl.pallas_call(
        _attn_out_kernel,
        grid=(nb, s // TOKEN_TILE),
        in_specs=[_tok_spec(D_MODEL)] + _post_specs(w_o, m, w_in, w_out, layer),
        out_specs=_tok_spec(D_MODEL),
        out_shape=jax.ShapeDtypeStruct(h.shape, F32),
        compiler_params=_cparams("parallel", "parallel"),
        name="attn_out_ffn",
    )(o, w_o, h, mod, gpre, gpost, w_in, w_out)


def _gla_out_kernel(of_ref, ob_ref, r_ref, og_ref, wo_ref, h_ref, mod_ref, gpre_ref, gpost_ref,
                    win_ref, wout_ref, out_ref):
    for rows in _sub_blocks(h_ref.shape[0]):
        o = of_ref[rows, :].astype(F32) + ob_ref[rows, :].astype(F32)
        normed = jnp.concatenate(
            [_rms(o[:, hd * GLA_DV:(hd + 1) * GLA_DV], og_ref[:, hd * GLA_DV:(hd + 1) * GLA_DV])
             for hd in range(GLA_HEADS)], axis=-1)
        z = (normed * _silu(r_ref[rows, :].astype(F32))).astype(BF16)
        h = _mix_residual(z, h_ref[rows, :], wo_ref, mod_ref, gpost_ref)
        out_ref[rows, :] = _swiglu_half(h, 2, mod_ref, gpre_ref, gpost_ref, win_ref, wout_ref)


def _gla_out(o_f, o_b, r, o_gain, w_o, m, h, mod, gpre, gpost, w_in, w_out, layer):
    nb, s, _ = h.shape
    return pl.pallas_call(
        _gla_out_kernel,
        grid=(nb, s // TOKEN_TILE),
        in_specs=[_tok_spec(GLA_HV)] * 3 + [_resident((1, GLA_HV))] + _post_specs(w_o, m, w_in, w_out, layer),
        out_specs=_tok_spec(D_MODEL),
        out_shape=jax.ShapeDtypeStruct(h.shape, F32),
        compiler_params=_cparams("parallel", "parallel"),
        name="gla_out_ffn",
    )(o_f, o_b, r, o_gain, w_o, h, mod, gpre, gpost, w_in, w_out)


def _chunk_cumsum(g, reverse):
    rows = g.shape[0]
    pos = lax.broadcasted_iota(jnp.int32, g.shape, 0) % GLA_CHUNK
    x = g
    step = 1
    while step < GLA_CHUNK:
        if reverse:
            x = x + jnp.where(pos < GLA_CHUNK - step, pltpu.roll(x, rows - step, 0), 0.0)
        else:
            x = x + jnp.where(pos >= step, pltpu.roll(x, step, 0), 0.0)
        step *= 2
    return x


def _gla_prep_kernel(h_ref, mod_ref, gpre_ref, win_ref, wa1_ref, wa2_ref, ba_ref,
                     qdf_ref, kif_ref, kef_ref, qdb_ref, kib_ref, keb_ref, v_ref, r_ref, dec_ref,
                     cum_ref):
    u = _modulated(h_ref[...], mod_ref, gpre_ref, 1)
    low = _dot(u, wa1_ref[...]).astype(BF16)
    gates = _log_sigmoid(_dot(low, wa2_ref[...]) + ba_ref[...]) / GLA_GATE_TAU
    proj = _dot(u, win_ref[...])
    q = proj[:, :GLA_HK] * (GLA_DK ** -0.5)
    k = proj[:, GLA_HK:2 * GLA_HK]
    v_ref[...] = proj[:, 2 * GLA_HK:2 * GLA_HK + GLA_HV].astype(BF16)
    r_ref[...] = proj[:, 2 * GLA_HK + GLA_HV:].astype(BF16)
    tm = q.shape[0]
    n_chunks = tm // GLA_CHUNK
    outs = ((qdf_ref, kif_ref, kef_ref), (qdb_ref, kib_ref, keb_ref))
    for d, (qd_ref, ki_ref, ke_ref) in enumerate(outs):
        cum = _chunk_cumsum(gates[:, d * GLA_HK:(d + 1) * GLA_HK], reverse=(d == 1))
        qd_ref[...] = (q * jnp.exp(cum)).astype(BF16)
        ki_ref[...] = (k * jnp.exp(-cum)).astype(BF16)
        edge = GLA_CHUNK - 1 if d == 0 else 0
        totals = []
        for hd in range(GLA_HEADS):
            cum_ref[hd] = cum[:, hd * GLA_DK:(hd + 1) * GLA_DK]
            totals.append(cum_ref[hd, pl.ds(edge, n_chunks, stride=GLA_CHUNK), :])
        total = jnp.concatenate(totals, axis=-1)
        dec_ref[:, d * GLA_HK:(d + 1) * GLA_HK] = jnp.exp(total)
        spread = jnp.broadcast_to(total[:, None, :], (n_chunks, GLA_CHUNK, GLA_HK)).reshape(tm, GLA_HK)
        ke_ref[...] = (k * jnp.exp(spread - cum)).astype(BF16)


def _gla_prep(h, mod, gpre, w_in, m, wa1, wa2, ba):
    nb, s, _ = h.shape
    nck = TOKEN_TILE // GLA_CHUNK
    half = jax.ShapeDtypeStruct((nb, s, GLA_HK), BF16)
    full = jax.ShapeDtypeStruct((nb, s, GLA_HV), BF16)
    return pl.pallas_call(
        _gla_prep_kernel,
        grid=(nb, s // TOKEN_TILE),
        in_specs=[_tok_spec(D_MODEL), _mod_spec(), _resident((3, D_MODEL)),
                  _resident_slice(w_in.shape, (m,)), _resident(wa1.shape), _resident(wa2.shape),
                  _resident(ba.shape)],
        out_specs=[_tok_spec(GLA_HK)] * 6 + [_tok_spec(GLA_HV)] * 2
        + [pl.BlockSpec((None, nck, 2 * GLA_HK), lambda b, t: (b, t, 0))],
        out_shape=[half] * 6 + [full] * 2
        + [jax.ShapeDtypeStruct((nb, s // GLA_CHUNK, 2 * GLA_HK), F32)],
        scratch_shapes=[pltpu.VMEM((GLA_HEADS, TOKEN_TILE, GLA_DK), F32)],
        compiler_params=_cparams("parallel", "parallel"),
        name="gla_prep",
    )(h, mod, gpre, w_in, wa1, wa2, ba)


def _gla_scan_kernel(n_blocks, qdf_ref, kif_ref, kef_ref, vf_ref, qdb_ref, kib_ref, keb_ref, vb_ref,
                     dec_ref, s0_ref, of_ref, ob_ref, st_ref):
    t = pl.program_id(1)

    @pl.when(t == 0)
    def _():
        st_ref[...] = s0_ref[...]

    n_chunks = qdf_ref.shape[0] // GLA_CHUNK
    row = lax.broadcasted_iota(jnp.int32, (GLA_CHUNK, GLA_CHUNK), 0)
    col = lax.broadcasted_iota(jnp.int32, (GLA_CHUNK, GLA_CHUNK), 1)
    dirs = ((qdf_ref, kif_ref, kef_ref, vf_ref, of_ref, col <= row),
            (qdb_ref, kib_ref, keb_ref, vb_ref, ob_ref, col >= row))
    for d, (qd_ref, ki_ref, ke_ref, v_ref, o_ref, mask) in enumerate(dirs):
        block = t if d == 0 else n_blocks - 1 - t
        for step in range(n_chunks):
            c = step if d == 0 else n_chunks - 1 - step
            rows = slice(c * GLA_CHUNK, (c + 1) * GLA_CHUNK)
            decay = dec_ref[pl.ds(block * n_chunks + c, 1), :]
            for hd in range(GLA_HEADS):
                kl = slice(hd * GLA_DK, (hd + 1) * GLA_DK)
                vl = slice(hd * GLA_DV, (hd + 1) * GLA_DV)
                qd, vv = qd_ref[rows, kl], v_ref[rows, vl]
                state = st_ref[d * GLA_HEADS + hd]
                scores = jnp.where(mask, _dot_nt(qd, ki_ref[rows, kl]), 0.0)
                o_ref[rows, vl] = (
                    _dot(scores.astype(BF16), vv) + _dot_nt(qd, state.astype(BF16))).astype(BF16)
                st_ref[d * GLA_HEADS + hd] = (
                    decay[:, d * GLA_HK + hd * GLA_DK:d * GLA_HK + (hd + 1) * GLA_DK] * state
                    + _dot_tn(vv, ke_ref[rows, kl]))


def _gla_scan(qdf, kif, kef, qdb, kib, keb, v, dec, state0):
    b, s, _ = v.shape
    n_blocks = s // SCAN_TOKENS
    fwd = lambda bi, t: (bi, t, 0)
    bwd = lambda bi, t: (bi, n_blocks - 1 - t, 0)
    hk = lambda m: pl.BlockSpec((None, SCAN_TOKENS, GLA_HK), m)
    hv = lambda m: pl.BlockSpec((None, SCAN_TOKENS, GLA_HV), m)
    st_spec = pl.BlockSpec((None, 2 * GLA_HEADS, GLA_DV, GLA_DK), lambda bi, t: (bi, 0, 0, 0))
    return pl.pallas_call(
        functools.partial(_gla_scan_kernel, n_blocks),
        grid=(b, n_blocks),
        in_specs=[hk(fwd), hk(fwd), hk(fwd), hv(fwd), hk(bwd), hk(bwd), hk(bwd), hv(bwd),
                  pl.BlockSpec((None, s // GLA_CHUNK, 2 * GLA_HK), lambda bi, t: (bi, 0, 0)),
                  st_spec],
        out_specs=[hv(fwd), hv(bwd), st_spec],
        out_shape=[jax.ShapeDtypeStruct((b, s, GLA_HV), BF16)] * 2
        + [jax.ShapeDtypeStruct(state0.shape, F32)],
        compiler_params=_cparams("parallel", "arbitrary"),
        name="gla_scan",
    )(qdf, kif, kef, v, qdb, kib, keb, v, dec, state0)


def _rope_tables(n_tok):
    pos = jnp.arange(n_tok, dtype=jnp.int32)
    inv = ROPE_THETA ** (-jnp.arange(0, ROPE_AXIS_DIM, 2, dtype=F32) / ROPE_AXIS_DIM)
    ang_r = (pos // GRID_W).astype(F32)[:, None] * inv[None, :]
    ang_c = (pos % GRID_W).astype(F32)[:, None] * inv[None, :]
    cos = jnp.concatenate([jnp.cos(ang_r)] * 2 + [jnp.cos(ang_c)] * 2, axis=-1)
    sin = jnp.concatenate([-jnp.sin(ang_r), jnp.sin(ang_r), -jnp.sin(ang_c), jnp.sin(ang_c)], axis=-1)
    return cos, sin


def kernel(x, c, ctx, c_ctx, ada_w, ada_b, norm_pre, norm_post, ffn_w_in, ffn_w_out, attn_w_qkv, attn_q_gain, attn_k_gain, attn_w_o, gla_w_in, gla_wa1, gla_wa2, gla_ba, gla_o_gain, gla_w_o):
    b, n_tok, _ = x.shape
    n_ctx = ctx.shape[1]
    assert b + 1 <= MOD_ROWS and n_tok % FFN_TILE == 0 and (b * n_ctx) % FFN_TILE == 0

    cc = jnp.concatenate([c, c_ctx[None, :], jnp.zeros((MOD_ROWS - b - 1, D_MODEL), F32)], axis=0)
    mods = _ada(cc, ada_w, ada_b).reshape(DEPTH, MOD_ROWS, N_MOD, D_MODEL)
    rope = _rope_tables(n_tok)
    w_in, w_out = ffn_w_in.astype(BF16), ffn_w_out.astype(BF16)
    w_qkv, w_attn_o = attn_w_qkv.astype(BF16), attn_w_o.astype(BF16)
    w_gla_in, w_gla_o = gla_w_in.astype(BF16), gla_w_o.astype(BF16)

    def per_batch(a):
        return a.reshape(b, n_ctx, a.shape[-1])

    hx = x
    hc = ctx.reshape(1, b * n_ctx, D_MODEL)
    for i in range(DEPTH):
        last = i == DEPTH - 1
        mod_x, mod_c = mods[i, :b], mods[i, b:b + 1]
        gpre, gpost = norm_pre[i], norm_post[i]

        hx = _ffn_pre(hx, mod_x, gpre, gpost, w_in, w_out, i)
        hc = _ffn_pre(hc, mod_c, gpre, gpost, w_in, w_out, i)

        m = i // 2
        if i % 2 == 0:
            qg, kg = attn_q_gain[m][None, :], attn_k_gain[m][None, :]
            qx, kx, vx = _attn_prep(hx, mod_x, gpre, w_qkv, m, qg, kg, rope)
            qc, kc, vc = (per_batch(a) for a in _attn_prep(hc, mod_c, gpre, w_qkv, m, qg, kg, None))
            ox = _attention(qx, [(kc, vc), (kx, vx)], ATTN_TQ)
            hx = _attn_out(ox, w_attn_o, m, hx, mod_x, gpre, gpost, w_in, w_out, i)
            if not last:
                oc = _attention(qc, [(kc, vc)], n_ctx)
                hc = _attn_out(oc.reshape(hc.shape), w_attn_o, m, hc, mod_c, gpre, gpost, w_in, w_out, i)
        else:
            wa1 = jnp.concatenate([gla_wa1[m, 0], gla_wa1[m, 1]], axis=-1).astype(BF16)
            zeros = jnp.zeros((GLA_GATE_RANK, GLA_HK), F32)
            wa2 = jnp.concatenate([jnp.concatenate([gla_wa2[m, 0], zeros], axis=-1),
                                   jnp.concatenate([zeros, gla_wa2[m, 1]], axis=-1)], axis=0).astype(BF16)
            ba = gla_ba[m].reshape(1, 2 * GLA_HK)
            o_gain = gla_o_gain[m][None, :]
            px = _gla_prep(hx, mod_x, gpre, w_gla_in, m, wa1, wa2, ba)
            pc = _gla_prep(hc, mod_c, gpre, w_gla_in, m, wa1, wa2, ba)
            pc = [per_batch(a) for a in pc[:8]] + [pc[8].reshape(b, n_ctx // GLA_CHUNK, 2 * GLA_HK)]
            zero_state = jnp.zeros((b, 2 * GLA_HEADS, GLA_DV, GLA_DK), F32)
            oc_f, oc_b, ctx_state = _gla_scan(*pc[:7], pc[8], zero_state)
            ox_f, ox_b, _ = _gla_scan(*px[:7], px[8], ctx_state)
            hx = _gla_out(ox_f, ox_b, px[7], o_gain, w_gla_o, m, hx, mod_x, gpre, gpost, w_in, w_out, i)
            if not last:
                hc = _gla_out(oc_f.reshape(1, b * n_ctx, GLA_HV), oc_b.reshape(1, b * n_ctx, GLA_HV),
                              pc[7].reshape(1, b * n_ctx, GLA_HV), o_gain, w_gla_o, m, hc, mod_c,
                              gpre, gpost, w_in, w_out, i)
    return hx
```

```python
import functools

import jax
import jax.numpy as jnp
from jax import lax
from jax.experimental import pallas as pl
from jax.experimental.pallas import tpu as pltpu

D_MODEL = 1024
DEPTH = 4
GRID_W = 64
N_MOD = 9
EPS = 1e-6
HEAD_DIM = 128
N_HEADS = 8
N_KV_HEADS = 2
Q_PER_KV = N_HEADS // N_KV_HEADS
ROPE_THETA = 10000.0
ROPE_AXIS_DIM = HEAD_DIM // 2
ROPE_HALF = ROPE_AXIS_DIM // 2
GLA_HEADS = 4
GLA_DK = 128
GLA_DV = 256
GLA_HK = GLA_HEADS * GLA_DK
GLA_HV = GLA_HEADS * GLA_DV
GLA_GATE_RANK = 16
GLA_GATE_TAU = 16.0
GLA_CHUNK = 64
D_FF = 2816

BF16 = jnp.bfloat16
F32 = jnp.float32

MOD_ROWS = 16
ADA_TN = 2304
TOKEN_TILE = 512
FFN_TILE = 1024
SUB_ROWS = 256
ATTN_TQ = 512
ATTN_KEY_CHUNK = 256
QK_PRESCALE = HEAD_DIM ** -0.5 * 1.4426950408889634
SCAN_TOKENS = 256
VMEM_LIMIT_BYTES = 56 * 1024 * 1024


def _cparams(*sem):
    return pltpu.CompilerParams(dimension_semantics=sem, vmem_limit_bytes=VMEM_LIMIT_BYTES)


def _resident(shape):
    nd = len(shape)
    return pl.BlockSpec(shape, lambda *_: (0,) * nd, pipeline_mode=pl.Buffered(1))


def _resident_slice(stacked_shape, lead):
    tail = tuple(stacked_shape[len(lead):])
    index = tuple(lead) + (0,) * len(tail)
    return pl.BlockSpec((None,) * len(lead) + tail, lambda *_: index, pipeline_mode=pl.Buffered(1))


def _dot(a, b):
    return jnp.dot(a, b, preferred_element_type=F32)


def _dot_nt(a, b):
    return lax.dot_general(a, b, (((1,), (1,)), ((), ())), preferred_element_type=F32)


def _dot_tn(a, b):
    return lax.dot_general(a, b, (((0,), (0,)), ((), ())), preferred_element_type=F32)


def _rms(x, g):
    return x * lax.rsqrt(jnp.mean(x * x, axis=-1, keepdims=True) + EPS) * g


def _silu(x):
    return x * jax.nn.sigmoid(x)


def _log_sigmoid(x):
    return jnp.minimum(x, 0.0) - jnp.log(1.0 + jnp.exp(-jnp.abs(x)))


def _modulated(h, mod_ref, gpre_ref, j):
    shift = mod_ref[3 * j:3 * j + 1, :]
    scale = mod_ref[3 * j + 1:3 * j + 2, :]
    return (_rms(h, gpre_ref[j:j + 1, :]) * (1.0 + scale) + shift).astype(BF16)


def _ada_kernel(c_ref, w_ref, b_ref, o_ref):
    s = _silu(c_ref[...]).astype(BF16)
    o_ref[...] = _dot(s, w_ref[...].astype(BF16)) + b_ref[...]


def _ada(cc, ada_w, ada_b):
    n = N_MOD * D_MODEL
    return pl.pallas_call(
        _ada_kernel,
        grid=(DEPTH, n // ADA_TN),
        in_specs=[
            _resident((MOD_ROWS, D_MODEL)),
            pl.BlockSpec((None, D_MODEL, ADA_TN), lambda l, t: (l, 0, t)),
            pl.BlockSpec((None, 1, ADA_TN), lambda l, t: (l, 0, t)),
        ],
        out_specs=pl.BlockSpec((None, MOD_ROWS, ADA_TN), lambda l, t: (l, 0, t)),
        out_shape=jax.ShapeDtypeStruct((DEPTH, MOD_ROWS, n), F32),
        compiler_params=_cparams("parallel", "parallel"),
        name="ada_mod",
    )(cc, ada_w, ada_b.reshape(DEPTH, 1, n))


def _sub_blocks(n_rows):
    return [slice(r0, r0 + SUB_ROWS) for r0 in range(0, n_rows, SUB_ROWS)]


def _pipelined_swiglu(load_h, n_rows, j, mod_ref, gpre_ref, gpost_ref, win_ref, wout_ref, o_ref):
    blocks = _sub_blocks(n_rows)
    n = len(blocks)
    hs, us, acts, ys = {}, {}, {}, {}

    def modulate(i):
        hs[i] = load_h(blocks[i])
        us[i] = _modulated(hs[i], mod_ref, gpre_ref, j)

    def gate_up(i):
        gu = _dot(us.pop(i), win_ref[...])
        acts[i] = (_silu(gu[:, :D_FF]) * gu[:, D_FF:]).astype(BF16)

    def down(i):
        ys[i] = _dot(acts.pop(i), wout_ref[...])

    def residual(i):
        o_ref[blocks[i], :] = hs.pop(i) + 0.5 * mod_ref[3 * j + 2:3 * j + 3, :] * _rms(
            ys.pop(i), gpost_ref[j:j + 1, :])

    modulate(0)
    gate_up(0)
    if n > 1:
        modulate(1)
    for i in range(n):
        down(i)
        if i + 1 < n:
            gate_up(i + 1)
        if i + 2 < n:
            modulate(i + 2)
        residual(i)


def _ffn_kernel(h_ref, mod_ref, gpre_ref, gpost_ref, win_ref, wout_ref, o_ref):
    _pipelined_swiglu(lambda rows: h_ref[rows, :], h_ref.shape[0], 0,
                      mod_ref, gpre_ref, gpost_ref, win_ref, wout_ref, o_ref)


def _tok_spec(width, tm=TOKEN_TILE):
    return pl.BlockSpec((None, tm, width), lambda b, t: (b, t, 0))


def _mod_spec():
    return pl.BlockSpec((None, N_MOD, D_MODEL), lambda b, t: (b, 0, 0))


def _ffn_pre(h, mod, gpre, gpost, w_in, w_out, layer):
    nb, s, _ = h.shape
    return pl.pallas_call(
        _ffn_kernel,
        grid=(nb, s // FFN_TILE),
        in_specs=[
            _tok_spec(D_MODEL, FFN_TILE), _mod_spec(),
            _resident((3, D_MODEL)), _resident((3, D_MODEL)),
            _resident_slice(w_in.shape, (layer, 0)), _resident_slice(w_out.shape, (layer, 0)),
        ],
        out_specs=_tok_spec(D_MODEL, FFN_TILE),
        out_shape=jax.ShapeDtypeStruct(h.shape, F32),
        compiler_params=_cparams("parallel", "parallel"),
        name="ffn_pre",
    )(h, mod, gpre, gpost, w_in, w_out)


def _attn_prep_kernel(use_rope, h_ref, mod_ref, gpre_ref, w_ref, qg_ref, kg_ref, *refs):
    if use_rope:
        cos_ref, sin_ref, q_ref, k_ref, v_ref = refs
    else:
        q_ref, k_ref, v_ref = refs
    k0 = N_HEADS * HEAD_DIM
    v0 = k0 + N_KV_HEADS * HEAD_DIM
    ones = jnp.ones((SUB_ROWS, HEAD_DIM), BF16)
    for rows in _sub_blocks(h_ref.shape[0]):
        u = _modulated(h_ref[rows, :], mod_ref, gpre_ref, 1)
        qkv = _dot(u, w_ref[...])

        def head(col, gain_ref, post_scale):
            x = _rms(qkv[:, col:col + HEAD_DIM], gain_ref[...])
            if use_rope:
                x = x * cos_ref[rows, :] + pltpu.roll(x, HEAD_DIM // 2, 1) * sin_ref[rows, :]
            if post_scale is not None:
                x = x * post_scale
            return x.astype(BF16)

        for hd in range(N_HEADS):
            q_ref[rows, hd * HEAD_DIM:(hd + 1) * HEAD_DIM] = head(hd * HEAD_DIM, qg_ref, QK_PRESCALE)
        for hd in range(N_KV_HEADS):
            k_ref[rows, hd * HEAD_DIM:(hd + 1) * HEAD_DIM] = head(k0 + hd * HEAD_DIM, kg_ref, None)
        for hd in range(N_KV_HEADS):
            v_ref[rows, 2 * hd * HEAD_DIM:(2 * hd + 1) * HEAD_DIM] = (
                qkv[:, v0 + hd * HEAD_DIM:v0 + (hd + 1) * HEAD_DIM].astype(BF16))
            v_ref[rows, (2 * hd + 1) * HEAD_DIM:(2 * hd + 2) * HEAD_DIM] = ones


def _attn_prep(h, mod, gpre, w_qkv, m, q_gain, k_gain, rope):
    nb, s, _ = h.shape
    kvw = N_KV_HEADS * HEAD_DIM
    in_specs = [
        _tok_spec(D_MODEL), _mod_spec(), _resident((3, D_MODEL)),
        _resident_slice(w_qkv.shape, (m,)),
        _resident((1, HEAD_DIM)), _resident((1, HEAD_DIM)),
    ]
    args = [h, mod, gpre, w_qkv, q_gain, k_gain]
    if rope is not None:
        in_specs += [pl.BlockSpec((TOKEN_TILE, HEAD_DIM), lambda b, t: (t, 0))] * 2
        args += list(rope)
    return pl.pallas_call(
        functools.partial(_attn_prep_kernel, rope is not None),
        grid=(nb, s // TOKEN_TILE),
        in_specs=in_specs,
        out_specs=[_tok_spec(D_MODEL), _tok_spec(kvw), _tok_spec(2 * kvw)],
        out_shape=[jax.ShapeDtypeStruct((nb, s, D_MODEL), BF16),
                   jax.ShapeDtypeStruct((nb, s, kvw), BF16),
                   jax.ShapeDtypeStruct((nb, s, 2 * kvw), BF16)],
        compiler_params=_cparams("parallel", "parallel"),
        name="attn_prep_rope" if rope is not None else "attn_prep_ctx",
    )(*args)


def _attention_kernel(chunks, q_ref, *refs):
    kv_refs, o_ref = refs[:-1], refs[-1]
    q = q_ref[...]
    qs = jnp.concatenate([q[:, g * HEAD_DIM:(g + 1) * HEAD_DIM] for g in range(Q_PER_KV)], axis=0)
    m = out = None
    for seg, lo, n in chunks:
        s = _dot_nt(qs, kv_refs[2 * seg][lo:lo + n, :])
        v = kv_refs[2 * seg + 1][lo:lo + n, :]
        row_max = jnp.max(s, axis=-1, keepdims=True)
        if m is None:
            m = row_max
            out = _dot(jnp.exp2(s - m).astype(BF16), v)
        else:
            m_new = jnp.maximum(m, row_max)
            out = jnp.exp2(m - m_new) * out + _dot(jnp.exp2(s - m_new).astype(BF16), v)
            m = m_new
    out = out[:, :HEAD_DIM] / out[:, HEAD_DIM:]
    tq = q.shape[0]
    for g in range(Q_PER_KV):
        o_ref[:, g * HEAD_DIM:(g + 1) * HEAD_DIM] = out[g * tq:(g + 1) * tq, :].astype(BF16)


def _attention(q, segments, tq):
    b, s, _ = q.shape
    gw = Q_PER_KV * HEAD_DIM
    in_specs = [pl.BlockSpec((None, tq, gw), lambda bi, kh, t: (bi, t, kh))]
    args = [q]
    chunks = []
    for seg, (k, v) in enumerate(segments):
        n_keys = k.shape[1]
        in_specs += [pl.BlockSpec((None, n_keys, HEAD_DIM), lambda bi, kh, t: (bi, 0, kh)),
                     pl.BlockSpec((None, n_keys, 2 * HEAD_DIM), lambda bi, kh, t: (bi, 0, kh))]
        args += [k, v]
        step = min(ATTN_KEY_CHUNK, n_keys)
        chunks += [(seg, lo, step) for lo in range(0, n_keys, step)]
    return pl.pallas_call(
        functools.partial(_attention_kernel, tuple(chunks)),
        grid=(b, N_KV_HEADS, s // tq),
        in_specs=in_specs,
        out_specs=pl.BlockSpec((None, tq, gw), lambda bi, kh, t: (bi, t, kh)),
        out_shape=jax.ShapeDtypeStruct(q.shape, BF16),
        compiler_params=_cparams("parallel", "parallel", "parallel"),
        name=f"attention_{len(segments)}seg",
    )(*args)


def _mix_residual(z, h, wo_ref, mod_ref, gpost_ref):
    return h + mod_ref[5:6, :] * _rms(_dot(z, wo_ref[...]), gpost_ref[1:2, :])


def _attn_out_kernel(o_ref, wo_ref, h_ref, mod_ref, gpre_ref, gpost_ref, win_ref, wout_ref, out_ref):
    def mixed(rows):
        return _mix_residual(o_ref[rows, :], h_ref[rows, :], wo_ref, mod_ref, gpost_ref)

    _pipelined_swiglu(mixed, h_ref.shape[0], 2, mod_ref, gpre_ref, gpost_ref, win_ref, wout_ref, out_ref)


def _post_specs(w_o, m, w_in, w_out, layer):
    return [_resident_slice(w_o.shape, (m,)), _tok_spec(D_MODEL), _mod_spec(),
            _resident((3, D_MODEL)), _resident((3, D_MODEL)),
            _resident_slice(w_in.shape, (layer, 1)), _resident_slice(w_out.shape, (layer, 1))]


def _attn_out(o, w_o, m, h, mod, gpre, gpost, w_in, w_out, layer):
    nb, s, _ = h.shape
    return pl.pallas_call(
        _attn_out_kernel,
        grid=(nb, s // TOKEN_TILE),
        in_specs=[_tok_spec(D_MODEL)] + _post_specs(w_o, m, w_in, w_out, layer),
        out_specs=_tok_spec(D_MODEL),
        out_shape=jax.ShapeDtypeStruct(h.shape, F32),
        compiler_params=_cparams("parallel", "parallel"),
        name="attn_out_ffn",
    )(o, w_o, h, mod, gpre, gpost, w_in, w_out)


def _gla_out_kernel(of_ref, ob_ref, r_ref, og_ref, wo_ref, h_ref, mod_ref, gpre_ref, gpost_ref,
                    win_ref, wout_ref, out_ref):
    def mixed(rows):
        o = of_ref[rows, :].astype(F32) + ob_ref[rows, :].astype(F32)
        normed = jnp.concatenate(
            [_rms(o[:, hd * GLA_DV:(hd + 1) * GLA_DV], og_ref[:, hd * GLA_DV:(hd + 1) * GLA_DV])
             for hd in range(GLA_HEADS)], axis=-1)
        z = (normed * _silu(r_ref[rows, :].astype(F32))).astype(BF16)
        return _mix_residual(z, h_ref[rows, :], wo_ref, mod_ref, gpost_ref)

    _pipelined_swiglu(mixed, h_ref.shape[0], 2, mod_ref, gpre_ref, gpost_ref, win_ref, wout_ref, out_ref)


def _gla_out(o_f, o_b, r, o_gain, w_o, m, h, mod, gpre, gpost, w_in, w_out, layer):
    nb, s, _ = h.shape
    return pl.pallas_call(
        _gla_out_kernel,
        grid=(nb, s // TOKEN_TILE),
        in_specs=[_tok_spec(GLA_HV)] * 3 + [_resident((1, GLA_HV))] + _post_specs(w_o, m, w_in, w_out, layer),
        out_specs=_tok_spec(D_MODEL),
        out_shape=jax.ShapeDtypeStruct(h.shape, F32),
        compiler_params=_cparams("parallel", "parallel"),
        name="gla_out_ffn",
    )(o_f, o_b, r, o_gain, w_o, h, mod, gpre, gpost, w_in, w_out)


def _chunk_cumsum(g, reverse):
    rows, width = g.shape
    pos = lax.broadcasted_iota(jnp.int32, g.shape, 0) % GLA_CHUNK
    x = g
    for step in (1, 2, 4):
        if reverse:
            x = x + jnp.where(pos < GLA_CHUNK - step, pltpu.roll(x, rows - step, 0), 0.0)
        else:
            x = x + jnp.where(pos >= step, pltpu.roll(x, step, 0), 0.0)
    x = x.reshape(rows // GLA_CHUNK, GLA_CHUNK, width)
    for step in (8, 16, 32):
        keep = GLA_CHUNK - step
        if reverse:
            x = jnp.concatenate([x[:, :keep] + x[:, step:], x[:, keep:]], axis=1)
        else:
            x = jnp.concatenate([x[:, :step], x[:, step:] + x[:, :keep]], axis=1)
    return x.reshape(rows, width)


def _gla_prep_kernel(h_ref, mod_ref, gpre_ref, win_ref, wa1_ref, wa2_ref, ba_ref,
                     qdf_ref, kif_ref, qdb_ref, kib_ref, v_ref, r_ref, dec_ref, cum_ref):
    n_chunks = SUB_ROWS // GLA_CHUNK
    outs = ((qdf_ref, kif_ref), (qdb_ref, kib_ref))
    for rows in _sub_blocks(h_ref.shape[0]):
        u = _modulated(h_ref[rows, :], mod_ref, gpre_ref, 1)
        low = _dot(u, wa1_ref[...]).astype(BF16)
        pre_gates = _dot(low, wa2_ref[...]) + ba_ref[...]
        proj = _dot(u, win_ref[...])
        v_ref[rows, :] = proj[:, 2 * GLA_HK:2 * GLA_HK + GLA_HV].astype(BF16)
        r_ref[rows, :] = proj[:, 2 * GLA_HK + GLA_HV:].astype(BF16)
        chunk0 = rows.start // GLA_CHUNK
        for d, (qd_ref, ki_ref) in enumerate(outs):
            edge = rows.start + (GLA_CHUNK - 1 if d == 0 else 0)
            for hd in range(GLA_HEADS):
                lanes = slice(hd * GLA_DK, (hd + 1) * GLA_DK)
                gate_lanes = slice(d * GLA_HK + hd * GLA_DK, d * GLA_HK + (hd + 1) * GLA_DK)
                cum = _chunk_cumsum(_log_sigmoid(pre_gates[:, gate_lanes]) / GLA_GATE_TAU, reverse=(d == 1))
                qd_ref[rows, lanes] = (proj[:, lanes] * (GLA_DK ** -0.5) * jnp.exp(cum)).astype(BF16)
                ki_ref[rows, lanes] = (proj[:, GLA_HK + hd * GLA_DK:GLA_HK + (hd + 1) * GLA_DK]
                                       * jnp.exp(-cum)).astype(BF16)
                cum_ref[d * GLA_HEADS + hd, rows, :] = cum
                total = cum_ref[d * GLA_HEADS + hd, pl.ds(edge, n_chunks, stride=GLA_CHUNK), :]
                dec_ref[chunk0:chunk0 + n_chunks, gate_lanes] = jnp.exp(total)


def _gla_prep(h, mod, gpre, w_in, m, wa1, wa2, ba):
    nb, s, _ = h.shape
    nck = TOKEN_TILE // GLA_CHUNK
    half = jax.ShapeDtypeStruct((nb, s, GLA_HK), BF16)
    full = jax.ShapeDtypeStruct((nb, s, GLA_HV), BF16)
    return pl.pallas_call(
        _gla_prep_kernel,
        grid=(nb, s // TOKEN_TILE),
        in_specs=[_tok_spec(D_MODEL), _mod_spec(), _resident((3, D_MODEL)),
                  _resident_slice(w_in.shape, (m,)), _resident(wa1.shape), _resident(wa2.shape),
                  _resident(ba.shape)],
        out_specs=[_tok_spec(GLA_HK)] * 4 + [_tok_spec(GLA_HV)] * 2
        + [pl.BlockSpec((None, nck, 2 * GLA_HK), lambda b, t: (b, t, 0))],
        out_shape=[half] * 4 + [full] * 2
        + [jax.ShapeDtypeStruct((nb, s // GLA_CHUNK, 2 * GLA_HK), F32)],
        scratch_shapes=[pltpu.VMEM((2 * GLA_HEADS, TOKEN_TILE, GLA_DK), F32)],
        compiler_params=_cparams("parallel", "parallel"),
        name="gla_prep",
    )(h, mod, gpre, w_in, wa1, wa2, ba)


def _gla_scan_kernel(n_blocks, qdf_ref, kif_ref, vf_ref, qdb_ref, kib_ref, vb_ref,
                     dec_ref, s0_ref, of_ref, ob_ref, st_ref):
    t = pl.program_id(1)

    @pl.when(t == 0)
    def _():
        st_ref[...] = s0_ref[...]

    n_chunks = qdf_ref.shape[0] // GLA_CHUNK
    row = lax.broadcasted_iota(jnp.int32, (GLA_CHUNK, GLA_CHUNK), 0)
    col = lax.broadcasted_iota(jnp.int32, (GLA_CHUNK, GLA_CHUNK), 1)
    dirs = ((qdf_ref, kif_ref, vf_ref, of_ref, col <= row),
            (qdb_ref, kib_ref, vb_ref, ob_ref, col >= row))
    for d, (qd_ref, ki_ref, v_ref, o_ref, mask) in enumerate(dirs):
        block = t if d == 0 else n_blocks - 1 - t
        for step in range(n_chunks):
            c = step if d == 0 else n_chunks - 1 - step
            rows = slice(c * GLA_CHUNK, (c + 1) * GLA_CHUNK)
            decay = dec_ref[pl.ds(block * n_chunks + c, 1), :]
            for hd in range(GLA_HEADS):
                kl = slice(hd * GLA_DK, (hd + 1) * GLA_DK)
                vl = slice(hd * GLA_DV, (hd + 1) * GLA_DV)
                qd, ki, vv = qd_ref[rows, kl], ki_ref[rows, kl], v_ref[rows, vl]
                state = st_ref[d * GLA_HEADS + hd]
                scores = jnp.where(mask, _dot_nt(qd, ki), 0.0)
                o_ref[rows, vl] = (
                    _dot(scores.astype(BF16), vv) + _dot_nt(qd, state.astype(BF16))).astype(BF16)
                st_ref[d * GLA_HEADS + hd] = (
                    decay[:, d * GLA_HK + hd * GLA_DK:d * GLA_HK + (hd + 1) * GLA_DK]
                    * (state + _dot_tn(vv, ki)))


def _gla_scan(qdf, kif, qdb, kib, v, dec, state0):
    b, s, _ = v.shape
    n_blocks = s // SCAN_TOKENS
    fwd = lambda bi, t: (bi, t, 0)
    bwd = lambda bi, t: (bi, n_blocks - 1 - t, 0)
    hk = lambda m: pl.BlockSpec((None, SCAN_TOKENS, GLA_HK), m)
    hv = lambda m: pl.BlockSpec((None, SCAN_TOKENS, GLA_HV), m)
    st_spec = pl.BlockSpec((None, 2 * GLA_HEADS, GLA_DV, GLA_DK), lambda bi, t: (bi, 0, 0, 0))
    return pl.pallas_call(
        functools.partial(_gla_scan_kernel, n_blocks),
        grid=(b, n_blocks),
        in_specs=[hk(fwd), hk(fwd), hv(fwd), hk(bwd), hk(bwd), hv(bwd),
                  pl.BlockSpec((None, s // GLA_CHUNK, 2 * GLA_HK), lambda bi, t: (bi, 0, 0)),
                  st_spec],
        out_specs=[hv(fwd), hv(bwd), st_spec],
        out_shape=[jax.ShapeDtypeStruct((b, s, GLA_HV), BF16)] * 2
        + [jax.ShapeDtypeStruct(state0.shape, F32)],
        compiler_params=_cparams("parallel", "arbitrary"),
        name="gla_scan",
    )(qdf, kif, v, qdb, kib, v, dec, state0)


def _rope_tables(n_tok):
    pos = jnp.arange(n_tok, dtype=jnp.int32)
    inv = ROPE_THETA ** (-jnp.arange(0, ROPE_AXIS_DIM, 2, dtype=F32) / ROPE_AXIS_DIM)
    ang_r = (pos // GRID_W).astype(F32)[:, None] * inv[None, :]
    ang_c = (pos % GRID_W).astype(F32)[:, None] * inv[None, :]
    cos = jnp.concatenate([jnp.cos(ang_r), jnp.cos(ang_c)] * 2, axis=-1)
    sin = jnp.concatenate([-jnp.sin(ang_r), -jnp.sin(ang_c), jnp.sin(ang_r), jnp.sin(ang_c)], axis=-1)
    return cos, sin


def _rope_head_layout():
    quarter = jnp.arange(ROPE_HALF)
    return jnp.concatenate([quarter, quarter + 2 * ROPE_HALF, quarter + ROPE_HALF, quarter + 3 * ROPE_HALF])


def kernel(x, c, ctx, c_ctx, ada_w, ada_b, norm_pre, norm_post, ffn_w_in, ffn_w_out, attn_w_qkv, attn_q_gain, attn_k_gain, attn_w_o, gla_w_in, gla_wa1, gla_wa2, gla_ba, gla_o_gain, gla_w_o):
    b, n_tok, _ = x.shape
    n_ctx = ctx.shape[1]
    assert b + 1 <= MOD_ROWS and n_tok % FFN_TILE == 0 and (b * n_ctx) % FFN_TILE == 0

    cc = jnp.concatenate([c, c_ctx[None, :], jnp.zeros((MOD_ROWS - b - 1, D_MODEL), F32)], axis=0)
    mods = _ada(cc, ada_w, ada_b).reshape(DEPTH, MOD_ROWS, N_MOD, D_MODEL)
    rope = _rope_tables(n_tok)
    w_in, w_out = ffn_w_in.astype(BF16), ffn_w_out.astype(BF16)
    head_perm = _rope_head_layout()
    n_qk = N_HEADS + N_KV_HEADS
    qk_cols = (jnp.arange(n_qk)[:, None] * HEAD_DIM + head_perm[None, :]).reshape(-1)
    cols = jnp.concatenate([qk_cols, jnp.arange(n_qk * HEAD_DIM, attn_w_qkv.shape[-1])])
    w_qkv, w_attn_o = attn_w_qkv[:, :, cols].astype(BF16), attn_w_o.astype(BF16)
    w_gla_in, w_gla_o = gla_w_in.astype(BF16), gla_w_o.astype(BF16)

    def per_batch(a):
        return a.reshape(b, n_ctx, a.shape[-1])

    hx = x
    hc = ctx.reshape(1, b * n_ctx, D_MODEL)
    for i in range(DEPTH):
        last = i == DEPTH - 1
        mod_x, mod_c = mods[i, :b], mods[i, b:b + 1]
        gpre, gpost = norm_pre[i], norm_post[i]

        hx = _ffn_pre(hx, mod_x, gpre, gpost, w_in, w_out, i)
        hc = _ffn_pre(hc, mod_c, gpre, gpost, w_in, w_out, i)

        m = i // 2
        if i % 2 == 0:
            qg, kg = attn_q_gain[m][head_perm][None, :], attn_k_gain[m][head_perm][None, :]
            qx, kx, vx = _attn_prep(hx, mod_x, gpre, w_qkv, m, qg, kg, rope)
            qc, kc, vc = (per_batch(a) for a in _attn_prep(hc, mod_c, gpre, w_qkv, m, qg, kg, None))
            ox = _attention(qx, [(kc, vc), (kx, vx)], ATTN_TQ)
            hx = _attn_out(ox, w_attn_o, m, hx, mod_x, gpre, gpost, w_in, w_out, i)
            if not last:
                oc = _attention(qc, [(kc, vc)], n_ctx)
                hc = _attn_out(oc.reshape(hc.shape), w_attn_o, m, hc, mod_c, gpre, gpost, w_in, w_out, i)
        else:
            wa1 = jnp.concatenate([gla_wa1[m, 0], gla_wa1[m, 1]], axis=-1).astype(BF16)
            zeros = jnp.zeros((GLA_GATE_RANK, GLA_HK), F32)
            wa2 = jnp.concatenate([jnp.concatenate([gla_wa2[m, 0], zeros], axis=-1),
                                   jnp.concatenate([zeros, gla_wa2[m, 1]], axis=-1)], axis=0).astype(BF16)
            ba = gla_ba[m].reshape(1, 2 * GLA_HK)
            o_gain = gla_o_gain[m][None, :]
            px = _gla_prep(hx, mod_x, gpre, w_gla_in, m, wa1, wa2, ba)
            pc = _gla_prep(hc, mod_c, gpre, w_gla_in, m, wa1, wa2, ba)
            pc = [per_batch(a) for a in pc[:6]] + [pc[6].reshape(b, n_ctx // GLA_CHUNK, 2 * GLA_HK)]
            zero_state = jnp.zeros((b, 2 * GLA_HEADS, GLA_DV, GLA_DK), F32)
            oc_f, oc_b, ctx_state = _gla_scan(*pc[:5], pc[6], zero_state)
            ox_f, ox_b, _ = _gla_scan(*px[:5], px[6], ctx_state)
            hx = _gla_out(ox_f, ox_b, px[5], o_gain, w_gla_o, m, hx, mod_x, gpre, gpost, w_in, w_out, i)
            if not last:
                hc = _gla_out(oc_f.reshape(1, b * n_ctx, GLA_HV), oc_b.reshape(1, b * n_ctx, GLA_HV),
                              pc[5].reshape(1, b * n_ctx, GLA_HV), o_gain, w_gla_o, m, hc, mod_c,
                              gpre, gpost, w_in, w_out, i)
    return hx
```

```python
import functools

import jax
import jax.numpy as jnp
from jax import lax
from jax.experimental import pallas as pl
from jax.experimental.pallas import tpu as pltpu

D_MODEL = 1024
DEPTH = 4
GRID_W = 64
N_MOD = 9
EPS = 1e-6
HEAD_DIM = 128
N_HEADS = 8
N_KV_HEADS = 2
Q_PER_KV = N_HEADS // N_KV_HEADS
ROPE_THETA = 10000.0
ROPE_AXIS_DIM = HEAD_DIM // 2
ROPE_HALF = ROPE_AXIS_DIM // 2
GLA_HEADS = 4
GLA_DK = 128
GLA_DV = 256
GLA_HK = GLA_HEADS * GLA_DK
GLA_HV = GLA_HEADS * GLA_DV
GLA_GATE_RANK = 16
GLA_GATE_TAU = 16.0
GLA_CHUNK = 64
D_FF = 2816

BF16 = jnp.bfloat16
F32 = jnp.float32

MOD_ROWS = 16
ADA_TN = 2304
TOKEN_TILE = 512
FFN_TILE = 1024
SUB_ROWS = 256
ATTN_TQ = 512
ATTN_KEY_CHUNK = 256
QK_PRESCALE = HEAD_DIM ** -0.5 * 1.4426950408889634
SCAN_TOKENS = 512
VMEM_LIMIT_BYTES = 56 * 1024 * 1024
BF16_SUBLANES = 16
CAST_BLOCK_BYTES = 6 * 1024 * 1024


def _cparams(*sem):
    return pltpu.CompilerParams(dimension_semantics=sem, vmem_limit_bytes=VMEM_LIMIT_BYTES)


def _resident(shape):
    nd = len(shape)
    return pl.BlockSpec(shape, lambda *_: (0,) * nd, pipeline_mode=pl.Buffered(1))


def _resident_slice(stacked_shape, lead):
    tail = tuple(stacked_shape[len(lead):])
    index = tuple(lead) + (0,) * len(tail)
    return pl.BlockSpec((None,) * len(lead) + tail, lambda *_: index, pipeline_mode=pl.Buffered(1))


def _dot(a, b):
    return jnp.dot(a, b, preferred_element_type=F32)


def _dot_nt(a, b):
    return lax.dot_general(a, b, (((1,), (1,)), ((), ())), preferred_element_type=F32)


def _dot_tn(a, b):
    return lax.dot_general(a, b, (((0,), (0,)), ((), ())), preferred_element_type=F32)


def _rms(x, g):
    return x * lax.rsqrt(jnp.mean(x * x, axis=-1, keepdims=True) + EPS) * g


def _silu(x):
    return x * jax.nn.sigmoid(x)


def _log_sigmoid(x):
    return jnp.minimum(x, 0.0) - jnp.log(1.0 + jnp.exp(-jnp.abs(x)))


def _modulated(h, mod_ref, gpre_ref, j):
    shift = mod_ref[3 * j:3 * j + 1, :]
    scale = mod_ref[3 * j + 1:3 * j + 2, :]
    return (_rms(h, gpre_ref[j:j + 1, :]) * (1.0 + scale) + shift).astype(BF16)


def _cast_kernel(x_ref, o_ref):
    o_ref[...] = x_ref[...].astype(o_ref.dtype)


def _to_bf16(w):
    width = w.shape[-1]
    flat = w.reshape(-1, width)
    n_rows = flat.shape[0]
    rows = max(r for r in range(BF16_SUBLANES, n_rows + 1, BF16_SUBLANES)
               if n_rows % r == 0 and r * width * 4 <= CAST_BLOCK_BYTES)
    out = pl.pallas_call(
        _cast_kernel,
        grid=(n_rows // rows,),
        in_specs=[pl.BlockSpec((rows, width), lambda i: (i, 0))],
        out_specs=pl.BlockSpec((rows, width), lambda i: (i, 0)),
        out_shape=jax.ShapeDtypeStruct(flat.shape, BF16),
        compiler_params=_cparams("parallel"),
        name="cast_bf16",
    )(flat)
    return out.reshape(w.shape)


def _ada_kernel(c_ref, w_ref, b_ref, o_ref):
    s = _silu(c_ref[...]).astype(BF16)
    o_ref[...] = _dot(s, w_ref[...].astype(BF16)) + b_ref[...]


def _ada(cc, ada_w, ada_b):
    n = N_MOD * D_MODEL
    return pl.pallas_call(
        _ada_kernel,
        grid=(DEPTH, n // ADA_TN),
        in_specs=[
            _resident((MOD_ROWS, D_MODEL)),
            pl.BlockSpec((None, D_MODEL, ADA_TN), lambda l, t: (l, 0, t)),
            pl.BlockSpec((None, 1, ADA_TN), lambda l, t: (l, 0, t)),
        ],
        out_specs=pl.BlockSpec((None, MOD_ROWS, ADA_TN), lambda l, t: (l, 0, t)),
        out_shape=jax.ShapeDtypeStruct((DEPTH, MOD_ROWS, n), F32),
        compiler_params=_cparams("parallel", "parallel"),
        name="ada_mod",
    )(cc, ada_w, ada_b.reshape(DEPTH, 1, n))


def _sub_blocks(n_rows):
    return [slice(r0, r0 + SUB_ROWS) for r0 in range(0, n_rows, SUB_ROWS)]


def _pipelined_swiglu(load_h, n_rows, j, mod_ref, gpre_ref, gpost_ref, win_ref, wout_ref, o_ref):
    blocks = _sub_blocks(n_rows)
    n = len(blocks)
    hs, us, acts, ys = {}, {}, {}, {}

    def modulate(i):
        hs[i] = load_h(blocks[i])
        us[i] = _modulated(hs[i], mod_ref, gpre_ref, j)

    def gate_up(i):
        gu = _dot(us.pop(i), win_ref[...])
        acts[i] = (_silu(gu[:, :D_FF]) * gu[:, D_FF:]).astype(BF16)

    def down(i):
        ys[i] = _dot(acts.pop(i), wout_ref[...])

    def residual(i):
        o_ref[blocks[i], :] = hs.pop(i) + 0.5 * mod_ref[3 * j + 2:3 * j + 3, :] * _rms(
            ys.pop(i), gpost_ref[j:j + 1, :])

    modulate(0)
    gate_up(0)
    if n > 1:
        modulate(1)
    for i in range(n):
        down(i)
        if i + 1 < n:
            gate_up(i + 1)
        if i + 2 < n:
            modulate(i + 2)
        residual(i)


def _ffn_kernel(h_ref, mod_ref, gpre_ref, gpost_ref, win_ref, wout_ref, o_ref):
    _pipelined_swiglu(lambda rows: h_ref[rows, :], h_ref.shape[0], 0,
                      mod_ref, gpre_ref, gpost_ref, win_ref, wout_ref, o_ref)


def _tok_spec(width, tm=TOKEN_TILE):
    return pl.BlockSpec((None, tm, width), lambda b, t: (b, t, 0))


def _mod_spec():
    return pl.BlockSpec((None, N_MOD, D_MODEL), lambda b, t: (b, 0, 0))


def _ffn_pre(h, mod, gpre, gpost, w_in, w_out, layer):
    nb, s, _ = h.shape
    return pl.pallas_call(
        _ffn_kernel,
        grid=(nb, s // FFN_TILE),
        in_specs=[
            _tok_spec(D_MODEL, FFN_TILE), _mod_spec(),
            _resident((3, D_MODEL)), _resident((3, D_MODEL)),
            _resident_slice(w_in.shape, (layer, 0)), _resident_slice(w_out.shape, (layer, 0)),
        ],
        out_specs=_tok_spec(D_MODEL, FFN_TILE),
        out_shape=jax.ShapeDtypeStruct(h.shape, F32),
        compiler_params=_cparams("parallel", "parallel"),
        name="ffn_pre",
    )(h, mod, gpre, gpost, w_in, w_out)


def _attn_prep_kernel(use_rope, h_ref, mod_ref, gpre_ref, w_ref, qg_ref, kg_ref, *refs):
    if use_rope:
        cos_ref, sin_ref, q_ref, k_ref, v_ref = refs
    else:
        q_ref, k_ref, v_ref = refs
    k0 = N_HEADS * HEAD_DIM
    v0 = k0 + N_KV_HEADS * HEAD_DIM
    ones = jnp.ones((SUB_ROWS, HEAD_DIM), BF16)
    for rows in _sub_blocks(h_ref.shape[0]):
        u = _modulated(h_ref[rows, :], mod_ref, gpre_ref, 1)
        qkv = _dot(u, w_ref[...])

        def head(col, gain_ref, post_scale):
            x = _rms(qkv[:, col:col + HEAD_DIM], gain_ref[...])
            if use_rope:
                x = x * cos_ref[rows, :] + pltpu.roll(x, HEAD_DIM // 2, 1) * sin_ref[rows, :]
            if post_scale is not None:
                x = x * post_scale
            return x.astype(BF16)

        for hd in range(N_HEADS):
            q_ref[rows, hd * HEAD_DIM:(hd + 1) * HEAD_DIM] = head(hd * HEAD_DIM, qg_ref, QK_PRESCALE)
        for hd in range(N_KV_HEADS):
            k_ref[rows, hd * HEAD_DIM:(hd + 1) * HEAD_DIM] = head(k0 + hd * HEAD_DIM, kg_ref, None)
        for hd in range(N_KV_HEADS):
            v_ref[rows, 2 * hd * HEAD_DIM:(2 * hd + 1) * HEAD_DIM] = (
                qkv[:, v0 + hd * HEAD_DIM:v0 + (hd + 1) * HEAD_DIM].astype(BF16))
            v_ref[rows, (2 * hd + 1) * HEAD_DIM:(2 * hd + 2) * HEAD_DIM] = ones


def _attn_prep(h, mod, gpre, w_qkv, m, q_gain, k_gain, rope):
    nb, s, _ = h.shape
    kvw = N_KV_HEADS * HEAD_DIM
    in_specs = [
        _tok_spec(D_MODEL), _mod_spec(), _resident((3, D_MODEL)),
        _resident_slice(w_qkv.shape, (m,)),
        _resident((1, HEAD_DIM)), _resident((1, HEAD_DIM)),
    ]
    args = [h, mod, gpre, w_qkv, q_gain, k_gain]
    if rope is not None:
        in_specs += [pl.BlockSpec((TOKEN_TILE, HEAD_DIM), lambda b, t: (t, 0))] * 2
        args += list(rope)
    return pl.pallas_call(
        functools.partial(_attn_prep_kernel, rope is not None),
        grid=(nb, s // TOKEN_TILE),
        in_specs=in_specs,
        out_specs=[_tok_spec(D_MODEL), _tok_spec(kvw), _tok_spec(2 * kvw)],
        out_shape=[jax.ShapeDtypeStruct((nb, s, D_MODEL), BF16),
                   jax.ShapeDtypeStruct((nb, s, kvw), BF16),
                   jax.ShapeDtypeStruct((nb, s, 2 * kvw), BF16)],
        compiler_params=_cparams("parallel", "parallel"),
        name="attn_prep_rope" if rope is not None else "attn_prep_ctx",
    )(*args)


def _attention_kernel(chunks, q_ref, *refs):
    kv_refs, o_ref = refs[:-1], refs[-1]
    q = q_ref[...]
    qs = jnp.concatenate([q[:, g * HEAD_DIM:(g + 1) * HEAD_DIM] for g in range(Q_PER_KV)], axis=0)
    m = out = None
    for seg, lo, n in chunks:
        s = _dot_nt(qs, kv_refs[2 * seg][lo:lo + n, :])
        v = kv_refs[2 * seg + 1][lo:lo + n, :]
        row_max = jnp.max(s, axis=-1, keepdims=True)
        if m is None:
            m = row_max
            out = _dot(jnp.exp2(s - m).astype(BF16), v)
        else:
            m_new = jnp.maximum(m, row_max)
            out = jnp.exp2(m - m_new) * out + _dot(jnp.exp2(s - m_new).astype(BF16), v)
            m = m_new
    out = out[:, :HEAD_DIM] / out[:, HEAD_DIM:]
    tq = q.shape[0]
    for g in range(Q_PER_KV):
        o_ref[:, g * HEAD_DIM:(g + 1) * HEAD_DIM] = out[g * tq:(g + 1) * tq, :].astype(BF16)


def _attention(q, segments, tq):
    b, s, _ = q.shape
    gw = Q_PER_KV * HEAD_DIM
    in_specs = [pl.BlockSpec((None, tq, gw), lambda bi, kh, t: (bi, t, kh))]
    args = [q]
    chunks = []
    for seg, (k, v) in enumerate(segments):
        n_keys = k.shape[1]
        in_specs += [pl.BlockSpec((None, n_keys, HEAD_DIM), lambda bi, kh, t: (bi, 0, kh)),
                     pl.BlockSpec((None, n_keys, 2 * HEAD_DIM), lambda bi, kh, t: (bi, 0, kh))]
        args += [k, v]
        step = min(ATTN_KEY_CHUNK, n_keys)
        chunks += [(seg, lo, step) for lo in range(0, n_keys, step)]
    return pl.pallas_call(
        functools.partial(_attention_kernel, tuple(chunks)),
        grid=(b, N_KV_HEADS, s // tq),
        in_specs=in_specs,
        out_specs=pl.BlockSpec((None, tq, gw), lambda bi, kh, t: (bi, t, kh)),
        out_shape=jax.ShapeDtypeStruct(q.shape, BF16),
        compiler_params=_cparams("parallel", "parallel", "parallel"),
        name=f"attention_{len(segments)}seg",
    )(*args)


def _mix_residual(z, h, wo_ref, mod_ref, gpost_ref):
    return h + mod_ref[5:6, :] * _rms(_dot(z, wo_ref[...]), gpost_ref[1:2, :])


def _attn_out_kernel(o_ref, wo_ref, h_ref, mod_ref, gpre_ref, gpost_ref, win_ref, wout_ref, out_ref):
    def mixed(rows):
        return _mix_residual(o_ref[rows, :], h_ref[rows, :], wo_ref, mod_ref, gpost_ref)

    _pipelined_swiglu(mixed, h_ref.shape[0], 2, mod_ref, gpre_ref, gpost_ref, win_ref, wout_ref, out_ref)


def _post_specs(w_o, m, w_in, w_out, layer):
    return [_resident_slice(w_o.shape, (m,)), _tok_spec(D_MODEL), _mod_spec(),
            _resident((3, D_MODEL)), _resident((3, D_MODEL)),
            _resident_slice(w_in.shape, (layer, 1)), _resident_slice(w_out.shape, (layer, 1))]


def _attn_out(o, w_o, m, h, mod, gpre, gpost, w_in, w_out, layer):
    nb, s, _ = h.shape
    return pl.pallas_call(
        _attn_out_kernel,
        grid=(nb, s // TOKEN_TILE),
        in_specs=[_tok_spec(D_MODEL)] + _post_specs(w_o, m, w_in, w_out, layer),
        out_specs=_tok_spec(D_MODEL),
        out_shape=jax.ShapeDtypeStruct(h.shape, F32),
        compiler_params=_cparams("parallel", "parallel"),
        name="attn_out_ffn",
    )(o, w_o, h, mod, gpre, gpost, w_in, w_out)


def _gla_out_kernel(of_ref, ob_ref, r_ref, og_ref, wo_ref, h_ref, mod_ref, gpre_ref, gpost_ref,
                    win_ref, wout_ref, out_ref):
    def mixed(rows):
        o = of_ref[rows, :].astype(F32) + ob_ref[rows, :].astype(F32)
        normed = jnp.concatenate(
            [_rms(o[:, hd * GLA_DV:(hd + 1) * GLA_DV], og_ref[:, hd * GLA_DV:(hd + 1) * GLA_DV])
             for hd in range(GLA_HEADS)], axis=-1)
        z = (normed * _silu(r_ref[rows, :].astype(F32))).astype(BF16)
        return _mix_residual(z, h_ref[rows, :], wo_ref, mod_ref, gpost_ref)

    _pipelined_swiglu(mixed, h_ref.shape[0], 2, mod_ref, gpre_ref, gpost_ref, win_ref, wout_ref, out_ref)


def _gla_out(o_f, o_b, r, o_gain, w_o, m, h, mod, gpre, gpost, w_in, w_out, layer):
    nb, s, _ = h.shape
    return pl.pallas_call(
        _gla_out_kernel,
        grid=(nb, s // TOKEN_TILE),
        in_specs=[_tok_spec(GLA_HV)] * 3 + [_resident((1, GLA_HV))] + _post_specs(w_o, m, w_in, w_out, layer),
        out_specs=_tok_spec(D_MODEL),
        out_shape=jax.ShapeDtypeStruct(h.shape, F32),
        compiler_params=_cparams("parallel", "parallel"),
        name="gla_out_ffn",
    )(o_f, o_b, r, o_gain, w_o, h, mod, gpre, gpost, w_in, w_out)


def _chunk_cumsum(g, reverse):
    rows, width = g.shape
    pos = lax.broadcasted_iota(jnp.int32, g.shape, 0) % GLA_CHUNK
    x = g
    for step in (1, 2, 4):
        if reverse:
            x = x + jnp.where(pos < GLA_CHUNK - step, pltpu.roll(x, rows - step, 0), 0.0)
        else:
            x = x + jnp.where(pos >= step, pltpu.roll(x, step, 0), 0.0)
    x = x.reshape(rows // GLA_CHUNK, GLA_CHUNK, width)
    for step in (8, 16, 32):
        keep = GLA_CHUNK - step
        if reverse:
            x = jnp.concatenate([x[:, :keep] + x[:, step:], x[:, keep:]], axis=1)
        else:
            x = jnp.concatenate([x[:, :step], x[:, step:] + x[:, :keep]], axis=1)
    return x.reshape(rows, width)


def _gla_prep_kernel(h_ref, mod_ref, gpre_ref, win_ref, wa1_ref, wa2_ref, ba_ref,
                     qdf_ref, kif_ref, qdb_ref, kib_ref, v_ref, r_ref, dec_ref, cum_ref):
    n_chunks = SUB_ROWS // GLA_CHUNK
    outs = ((qdf_ref, kif_ref), (qdb_ref, kib_ref))
    for rows in _sub_blocks(h_ref.shape[0]):
        u = _modulated(h_ref[rows, :], mod_ref, gpre_ref, 1)
        low = _dot(u, wa1_ref[...]).astype(BF16)
        pre_gates = _dot(low, wa2_ref[...]) + ba_ref[...]
        proj = _dot(u, win_ref[...])
        v_ref[rows, :] = proj[:, 2 * GLA_HK:2 * GLA_HK + GLA_HV].astype(BF16)
        r_ref[rows, :] = proj[:, 2 * GLA_HK + GLA_HV:].astype(BF16)
        chunk0 = rows.start // GLA_CHUNK
        for d, (qd_ref, ki_ref) in enumerate(outs):
            edge = rows.start + (GLA_CHUNK - 1 if d == 0 else 0)
            for hd in range(GLA_HEADS):
                lanes = slice(hd * GLA_DK, (hd + 1) * GLA_DK)
                gate_lanes = slice(d * GLA_HK + hd * GLA_DK, d * GLA_HK + (hd + 1) * GLA_DK)
                cum = _chunk_cumsum(_log_sigmoid(pre_gates[:, gate_lanes]) / GLA_GATE_TAU, reverse=(d == 1))
                qd_ref[rows, lanes] = (proj[:, lanes] * (GLA_DK ** -0.5) * jnp.exp(cum)).astype(BF16)
                ki_ref[rows, lanes] = (proj[:, GLA_HK + hd * GLA_DK:GLA_HK + (hd + 1) * GLA_DK]
                                       * jnp.exp(-cum)).astype(BF16)
                cum_ref[d * GLA_HEADS + hd, rows, :] = cum
                total = cum_ref[d * GLA_HEADS + hd, pl.ds(edge, n_chunks, stride=GLA_CHUNK), :]
                dec_ref[chunk0:chunk0 + n_chunks, gate_lanes] = jnp.exp(total)


def _gla_prep(h, mod, gpre, w_in, m, wa1, wa2, ba):
    nb, s, _ = h.shape
    nck = TOKEN_TILE // GLA_CHUNK
    half = jax.ShapeDtypeStruct((nb, s, GLA_HK), BF16)
    full = jax.ShapeDtypeStruct((nb, s, GLA_HV), BF16)
    return pl.pallas_call(
        _gla_prep_kernel,
        grid=(nb, s // TOKEN_TILE),
        in_specs=[_tok_spec(D_MODEL), _mod_spec(), _resident((3, D_MODEL)),
                  _resident_slice(w_in.shape, (m,)), _resident(wa1.shape), _resident(wa2.shape),
                  _resident(ba.shape)],
        out_specs=[_tok_spec(GLA_HK)] * 4 + [_tok_spec(GLA_HV)] * 2
        + [pl.BlockSpec((None, nck, 2 * GLA_HK), lambda b, t: (b, t, 0))],
        out_shape=[half] * 4 + [full] * 2
        + [jax.ShapeDtypeStruct((nb, s // GLA_CHUNK, 2 * GLA_HK), F32)],
        scratch_shapes=[pltpu.VMEM((2 * GLA_HEADS, TOKEN_TILE, GLA_DK), F32)],
        compiler_params=_cparams("parallel", "parallel"),
        name="gla_prep",
    )(h, mod, gpre, w_in, wa1, wa2, ba)


def _gla_scan_kernel(n_blocks, qdf_ref, kif_ref, vf_ref, qdb_ref, kib_ref, vb_ref,
                     dec_ref, s0_ref, of_ref, ob_ref, st_ref):
    t = pl.program_id(1)

    @pl.when(t == 0)
    def _():
        st_ref[...] = s0_ref[...]

    n_chunks = qdf_ref.shape[0] // GLA_CHUNK
    row = lax.broadcasted_iota(jnp.int32, (GLA_CHUNK, GLA_CHUNK), 0)
    col = lax.broadcasted_iota(jnp.int32, (GLA_CHUNK, GLA_CHUNK), 1)
    dirs = ((qdf_ref, kif_ref, vf_ref, of_ref, col <= row),
            (qdb_ref, kib_ref, vb_ref, ob_ref, col >= row))
    for d, (qd_ref, ki_ref, v_ref, o_ref, mask) in enumerate(dirs):
        block = t if d == 0 else n_blocks - 1 - t
        for step in range(n_chunks):
            c = step if d == 0 else n_chunks - 1 - step
            rows = slice(c * GLA_CHUNK, (c + 1) * GLA_CHUNK)
            decay = dec_ref[pl.ds(block * n_chunks + c, 1), :]
            for hd in range(GLA_HEADS):
                kl = slice(hd * GLA_DK, (hd + 1) * GLA_DK)
                vl = slice(hd * GLA_DV, (hd + 1) * GLA_DV)
                qd, ki, vv = qd_ref[rows, kl], ki_ref[rows, kl], v_ref[rows, vl]
                state = st_ref[d * GLA_HEADS + hd]
                scores = jnp.where(mask, _dot_nt(qd, ki), 0.0)
                o_ref[rows, vl] = (
                    _dot(scores.astype(BF16), vv) + _dot_nt(qd, state.astype(BF16))).astype(BF16)
                st_ref[d * GLA_HEADS + hd] = (
                    decay[:, d * GLA_HK + hd * GLA_DK:d * GLA_HK + (hd + 1) * GLA_DK]
                    * (state + _dot_tn(vv, ki)))


def _gla_scan(qdf, kif, qdb, kib, v, dec, state0):
    b, s, _ = v.shape
    tokens = min(SCAN_TOKENS, s)
    n_blocks = s // tokens
    fwd = lambda bi, t: (bi, t, 0)
    bwd = lambda bi, t: (bi, n_blocks - 1 - t, 0)
    hk = lambda m: pl.BlockSpec((None, tokens, GLA_HK), m)
    hv = lambda m: pl.BlockSpec((None, tokens, GLA_HV), m)
    st_spec = pl.BlockSpec((None, 2 * GLA_HEADS, GLA_DV, GLA_DK), lambda bi, t: (bi, 0, 0, 0))
    return pl.pallas_call(
        functools.partial(_gla_scan_kernel, n_blocks),
        grid=(b, n_blocks),
        in_specs=[hk(fwd), hk(fwd), hv(fwd), hk(bwd), hk(bwd), hv(bwd),
                  pl.BlockSpec((None, s // GLA_CHUNK, 2 * GLA_HK), lambda bi, t: (bi, 0, 0)),
                  st_spec],
        out_specs=[hv(fwd), hv(bwd), st_spec],
        out_shape=[jax.ShapeDtypeStruct((b, s, GLA_HV), BF16)] * 2
        + [jax.ShapeDtypeStruct(state0.shape, F32)],
        compiler_params=_cparams("parallel", "arbitrary"),
        name="gla_scan",
    )(qdf, kif, v, qdb, kib, v, dec, state0)


def _rope_tables(n_tok):
    pos = jnp.arange(n_tok, dtype=jnp.int32)
    inv = ROPE_THETA ** (-jnp.arange(0, ROPE_AXIS_DIM, 2, dtype=F32) / ROPE_AXIS_DIM)
    ang_r = (pos // GRID_W).astype(F32)[:, None] * inv[None, :]
    ang_c = (pos % GRID_W).astype(F32)[:, None] * inv[None, :]
    cos = jnp.concatenate([jnp.cos(ang_r), jnp.cos(ang_c)] * 2, axis=-1)
    sin = jnp.concatenate([-jnp.sin(ang_r), -jnp.sin(ang_c), jnp.sin(ang_r), jnp.sin(ang_c)], axis=-1)
    return cos, sin


def _rope_head_layout(a):
    lead = a.shape[:-1]
    a = a.reshape(lead + (-1, 2, 2, ROPE_HALF))
    return jnp.swapaxes(a, -2, -3).reshape(lead + (-1,))


def kernel(x, c, ctx, c_ctx, ada_w, ada_b, norm_pre, norm_post, ffn_w_in, ffn_w_out, attn_w_qkv, attn_q_gain, attn_k_gain, attn_w_o, gla_w_in, gla_wa1, gla_wa2, gla_ba, gla_o_gain, gla_w_o):
    b, n_tok, _ = x.shape
    n_ctx = ctx.shape[1]
    assert b + 1 <= MOD_ROWS and n_tok % FFN_TILE == 0 and (b * n_ctx) % FFN_TILE == 0

    cc = jnp.concatenate([c, c_ctx[None, :], jnp.zeros((MOD_ROWS - b - 1, D_MODEL), F32)], axis=0)
    mods = _ada(cc, ada_w, ada_b).reshape(DEPTH, MOD_ROWS, N_MOD, D_MODEL)
    rope = _rope_tables(n_tok)
    w_in, w_out = _to_bf16(ffn_w_in), _to_bf16(ffn_w_out)
    w_qkv, w_attn_o = _to_bf16(attn_w_qkv), _to_bf16(attn_w_o)
    w_gla_in, w_gla_o = _to_bf16(gla_w_in), _to_bf16(gla_w_o)
    qk_width = (N_HEADS + N_KV_HEADS) * HEAD_DIM
    w_qkv = jnp.concatenate([_rope_head_layout(w_qkv[..., :qk_width]), w_qkv[..., qk_width:]], axis=-1)

    def per_batch(a):
        return a.reshape(b, n_ctx, a.shape[-1])

    hx = x
    hc = ctx.reshape(1, b * n_ctx, D_MODEL)
    for i in range(DEPTH):
        last = i == DEPTH - 1
        mod_x, mod_c = mods[i, :b], mods[i, b:b + 1]
        gpre, gpost = norm_pre[i], norm_post[i]

        hx = _ffn_pre(hx, mod_x, gpre, gpost, w_in, w_out, i)
        hc = _ffn_pre(hc, mod_c, gpre, gpost, w_in, w_out, i)

        m = i // 2
        if i % 2 == 0:
            qg = _rope_head_layout(attn_q_gain[m])[None, :]
            kg = _rope_head_layout(attn_k_gain[m])[None, :]
            qx, kx, vx = _attn_prep(hx, mod_x, gpre, w_qkv, m, qg, kg, rope)
            qc, kc, vc = (per_batch(a) for a in _attn_prep(hc, mod_c, gpre, w_qkv, m, qg, kg, None))
            ox = _attention(qx, [(kc, vc), (kx, vx)], ATTN_TQ)
            hx = _attn_out(ox, w_attn_o, m, hx, mod_x, gpre, gpost, w_in, w_out, i)
            if not last:
                oc = _attention(qc, [(kc, vc)], n_ctx)
                hc = _attn_out(oc.reshape(hc.shape), w_attn_o, m, hc, mod_c, gpre, gpost, w_in, w_out, i)
        else:
            wa1 = jnp.concatenate([gla_wa1[m, 0], gla_wa1[m, 1]], axis=-1).astype(BF16)
            zeros = jnp.zeros((GLA_GATE_RANK, GLA_HK), F32)
            wa2 = jnp.concatenate([jnp.concatenate([gla_wa2[m, 0], zeros], axis=-1),
                                   jnp.concatenate([zeros, gla_wa2[m, 1]], axis=-1)], axis=0).astype(BF16)
            ba = gla_ba[m].reshape(1, 2 * GLA_HK)
            o_gain = gla_o_gain[m][None, :]
            px = _gla_prep(hx, mod_x, gpre, w_gla_in, m, wa1, wa2, ba)
            pc = _gla_prep(hc, mod_c, gpre, w_gla_in, m, wa1, wa2, ba)
            pc = [per_batch(a) for a in pc[:6]] + [pc[6].reshape(b, n_ctx // GLA_CHUNK, 2 * GLA_HK)]
            zero_state = jnp.zeros((b, 2 * GLA_HEADS, GLA_DV, GLA_DK), F32)
            oc_f, oc_b, ctx_state = _gla_scan(*pc[:5], pc[6], zero_state)
            ox_f, ox_b, _ = _gla_scan(*px[:5], px[6], ctx_state)
            hx = _gla_out(ox_f, ox_b, px[5], o_gain, w_gla_o, m, hx, mod_x, gpre, gpost, w_in, w_out, i)
            if not last:
                hc = _gla_out(oc_f.reshape(1, b * n_ctx, GLA_HV), oc_b.reshape(1, b * n_ctx, GLA_HV),
                              pc[5].reshape(1, b * n_ctx, GLA_HV), o_gain, w_gla_o, m, hc, mod_c,
                              gpre, gpost, w_in, w_out, i)
    return hx
```

```python
import functools

import jax
import jax.numpy as jnp
from jax import lax
from jax.experimental import pallas as pl
from jax.experimental.pallas import tpu as pltpu

D_MODEL = 1024
DEPTH = 4
GRID_W = 64
N_MOD = 9
EPS = 1e-6
HEAD_DIM = 128
N_HEADS = 8
N_KV_HEADS = 2
Q_PER_KV = N_HEADS // N_KV_HEADS
ROPE_THETA = 10000.0
ROPE_AXIS_DIM = HEAD_DIM // 2
ROPE_HALF = ROPE_AXIS_DIM // 2
GLA_HEADS = 4
GLA_DK = 128
GLA_DV = 256
GLA_HK = GLA_HEADS * GLA_DK
GLA_HV = GLA_HEADS * GLA_DV
GLA_GATE_RANK = 16
GLA_GATE_TAU = 16.0
GLA_CHUNK = 64
D_FF = 2816

BF16 = jnp.bfloat16
F32 = jnp.float32

MOD_ROWS = 16
ADA_TN = 2304
TOKEN_TILE = 512
FFN_TILE = 1024
SUB_ROWS = 256
ATTN_TQ = 512
ATTN_KEY_CHUNK = 256
QK_PRESCALE = HEAD_DIM ** -0.5 * 1.4426950408889634
SCAN_TOKENS = 1024
VMEM_LIMIT_BYTES = 56 * 1024 * 1024
BF16_SUBLANES = 16
CAST_BLOCK_BYTES = 6 * 1024 * 1024


def _cparams(*sem):
    return pltpu.CompilerParams(dimension_semantics=sem, vmem_limit_bytes=VMEM_LIMIT_BYTES)


def _resident(shape):
    nd = len(shape)
    return pl.BlockSpec(shape, lambda *_: (0,) * nd, pipeline_mode=pl.Buffered(1))


def _resident_slice(stacked_shape, lead):
    tail = tuple(stacked_shape[len(lead):])
    index = tuple(lead) + (0,) * len(tail)
    return pl.BlockSpec((None,) * len(lead) + tail, lambda *_: index, pipeline_mode=pl.Buffered(1))


def _dot(a, b):
    return jnp.dot(a, b, preferred_element_type=F32)


def _dot_nt(a, b):
    return lax.dot_general(a, b, (((1,), (1,)), ((), ())), preferred_element_type=F32)


def _dot_tn(a, b):
    return lax.dot_general(a, b, (((0,), (0,)), ((), ())), preferred_element_type=F32)


def _rms(x, g):
    return x * lax.rsqrt(jnp.mean(x * x, axis=-1, keepdims=True) + EPS) * g


def _silu(x):
    return x * jax.nn.sigmoid(x)


def _log_sigmoid(x):
    return jnp.minimum(x, 0.0) - jnp.log(1.0 + jnp.exp(-jnp.abs(x)))


def _modulated(h, mod_ref, gpre_ref, j):
    shift = mod_ref[3 * j:3 * j + 1, :]
    scale = mod_ref[3 * j + 1:3 * j + 2, :]
    return (_rms(h, gpre_ref[j:j + 1, :]) * (1.0 + scale) + shift).astype(BF16)


def _cast_kernel(x_ref, o_ref):
    o_ref[...] = x_ref[...].astype(o_ref.dtype)


def _to_bf16(w):
    width = w.shape[-1]
    flat = w.reshape(-1, width)
    n_rows = flat.shape[0]
    rows = max(r for r in range(BF16_SUBLANES, n_rows + 1, BF16_SUBLANES)
               if n_rows % r == 0 and r * width * 4 <= CAST_BLOCK_BYTES)
    out = pl.pallas_call(
        _cast_kernel,
        grid=(n_rows // rows,),
        in_specs=[pl.BlockSpec((rows, width), lambda i: (i, 0))],
        out_specs=pl.BlockSpec((rows, width), lambda i: (i, 0)),
        out_shape=jax.ShapeDtypeStruct(flat.shape, BF16),
        compiler_params=_cparams("parallel"),
        name="cast_bf16",
    )(flat)
    return out.reshape(w.shape)


def _ada_kernel(c_ref, w_ref, b_ref, o_ref):
    s = _silu(c_ref[...]).astype(BF16)
    o_ref[...] = _dot(s, w_ref[...].astype(BF16)) + b_ref[...]


def _ada(cc, ada_w, ada_b):
    n = N_MOD * D_MODEL
    return pl.pallas_call(
        _ada_kernel,
        grid=(DEPTH, n // ADA_TN),
        in_specs=[
            _resident((MOD_ROWS, D_MODEL)),
            pl.BlockSpec((None, D_MODEL, ADA_TN), lambda l, t: (l, 0, t)),
            pl.BlockSpec((None, 1, ADA_TN), lambda l, t: (l, 0, t)),
        ],
        out_specs=pl.BlockSpec((None, MOD_ROWS, ADA_TN), lambda l, t: (l, 0, t)),
        out_shape=jax.ShapeDtypeStruct((DEPTH, MOD_ROWS, n), F32),
        compiler_params=_cparams("parallel", "parallel"),
        name="ada_mod",
    )(cc, ada_w, ada_b.reshape(DEPTH, 1, n))


def _sub_blocks(n_rows):
    return [slice(r0, r0 + SUB_ROWS) for r0 in range(0, n_rows, SUB_ROWS)]


def _pipelined_swiglu(load_h, n_rows, j, mod_ref, gpre_ref, gpost_ref, win_ref, wout_ref, o_ref):
    blocks = _sub_blocks(n_rows)
    n = len(blocks)
    hs, us, acts, ys = {}, {}, {}, {}

    def modulate(i):
        hs[i] = load_h(blocks[i])
        us[i] = _modulated(hs[i], mod_ref, gpre_ref, j)

    def gate_up(i):
        gu = _dot(us.pop(i), win_ref[...])
        acts[i] = (_silu(gu[:, :D_FF]) * gu[:, D_FF:]).astype(BF16)

    def down(i):
        ys[i] = _dot(acts.pop(i), wout_ref[...])

    def residual(i):
        o_ref[blocks[i], :] = hs.pop(i) + 0.5 * mod_ref[3 * j + 2:3 * j + 3, :] * _rms(
            ys.pop(i), gpost_ref[j:j + 1, :])

    modulate(0)
    gate_up(0)
    if n > 1:
        modulate(1)
    for i in range(n):
        down(i)
        if i + 1 < n:
            gate_up(i + 1)
        if i + 2 < n:
            modulate(i + 2)
        residual(i)


def _ffn_kernel(h_ref, mod_ref, gpre_ref, gpost_ref, win_ref, wout_ref, o_ref):
    _pipelined_swiglu(lambda rows: h_ref[rows, :], h_ref.shape[0], 0,
                      mod_ref, gpre_ref, gpost_ref, win_ref, wout_ref, o_ref)


def _tok_spec(width, tm=TOKEN_TILE):
    return pl.BlockSpec((None, tm, width), lambda b, t: (b, t, 0))


def _mod_spec():
    return pl.BlockSpec((None, N_MOD, D_MODEL), lambda b, t: (b, 0, 0))


def _ffn_pre(h, mod, gpre, gpost, w_in, w_out, layer):
    nb, s, _ = h.shape
    return pl.pallas_call(
        _ffn_kernel,
        grid=(nb, s // FFN_TILE),
        in_specs=[
            _tok_spec(D_MODEL, FFN_TILE), _mod_spec(),
            _resident((3, D_MODEL)), _resident((3, D_MODEL)),
            _resident_slice(w_in.shape, (layer, 0)), _resident_slice(w_out.shape, (layer, 0)),
        ],
        out_specs=_tok_spec(D_MODEL, FFN_TILE),
        out_shape=jax.ShapeDtypeStruct(h.shape, F32),
        compiler_params=_cparams("parallel", "parallel"),
        name="ffn_pre",
    )(h, mod, gpre, gpost, w_in, w_out)


def _attn_prep_kernel(use_rope, h_ref, mod_ref, gpre_ref, w_ref, qg_ref, kg_ref, *refs):
    if use_rope:
        cos_ref, sin_ref, q_ref, k_ref, v_ref = refs
    else:
        q_ref, k_ref, v_ref = refs
    k0 = N_HEADS * HEAD_DIM
    v0 = k0 + N_KV_HEADS * HEAD_DIM
    ones = jnp.ones((SUB_ROWS, HEAD_DIM), BF16)
    for rows in _sub_blocks(h_ref.shape[0]):
        u = _modulated(h_ref[rows, :], mod_ref, gpre_ref, 1)
        qkv = _dot(u, w_ref[...])

        def head(col, gain_ref, post_scale):
            x = _rms(qkv[:, col:col + HEAD_DIM], gain_ref[...])
            if use_rope:
                x = x * cos_ref[rows, :] + pltpu.roll(x, HEAD_DIM // 2, 1) * sin_ref[rows, :]
            if post_scale is not None:
                x = x * post_scale
            return x.astype(BF16)

        for hd in range(N_HEADS):
            q_ref[rows, hd * HEAD_DIM:(hd + 1) * HEAD_DIM] = head(hd * HEAD_DIM, qg_ref, QK_PRESCALE)
        for hd in range(N_KV_HEADS):
            k_ref[rows, hd * HEAD_DIM:(hd + 1) * HEAD_DIM] = head(k0 + hd * HEAD_DIM, kg_ref, None)
        for hd in range(N_KV_HEADS):
            v_ref[rows, 2 * hd * HEAD_DIM:(2 * hd + 1) * HEAD_DIM] = (
                qkv[:, v0 + hd * HEAD_DIM:v0 + (hd + 1) * HEAD_DIM].astype(BF16))
            v_ref[rows, (2 * hd + 1) * HEAD_DIM:(2 * hd + 2) * HEAD_DIM] = ones


def _attn_prep(h, mod, gpre, w_qkv, m, q_gain, k_gain, rope):
    nb, s, _ = h.shape
    kvw = N_KV_HEADS * HEAD_DIM
    in_specs = [
        _tok_spec(D_MODEL), _mod_spec(), _resident((3, D_MODEL)),
        _resident_slice(w_qkv.shape, (m,)),
        _resident((1, HEAD_DIM)), _resident((1, HEAD_DIM)),
    ]
    args = [h, mod, gpre, w_qkv, q_gain, k_gain]
    if rope is not None:
        in_specs += [pl.BlockSpec((TOKEN_TILE, HEAD_DIM), lambda b, t: (t, 0))] * 2
        args += list(rope)
    return pl.pallas_call(
        functools.partial(_attn_prep_kernel, rope is not None),
        grid=(nb, s // TOKEN_TILE),
        in_specs=in_specs,
        out_specs=[_tok_spec(D_MODEL), _tok_spec(kvw), _tok_spec(2 * kvw)],
        out_shape=[jax.ShapeDtypeStruct((nb, s, D_MODEL), BF16),
                   jax.ShapeDtypeStruct((nb, s, kvw), BF16),
                   jax.ShapeDtypeStruct((nb, s, 2 * kvw), BF16)],
        compiler_params=_cparams("parallel", "parallel"),
        name="attn_prep_rope" if rope is not None else "attn_prep_ctx",
    )(*args)


def _attention_kernel(chunks, q_ref, *refs):
    kv_refs, o_ref = refs[:-1], refs[-1]
    q = q_ref[...]
    qs = jnp.concatenate([q[:, g * HEAD_DIM:(g + 1) * HEAD_DIM] for g in range(Q_PER_KV)], axis=0)
    m = out = None
    for seg, lo, n in chunks:
        s = _dot_nt(qs, kv_refs[2 * seg][lo:lo + n, :])
        v = kv_refs[2 * seg + 1][lo:lo + n, :]
        row_max = jnp.max(s, axis=-1, keepdims=True)
        if m is None:
            m = row_max
            out = _dot(jnp.exp2(s - m).astype(BF16), v)
        else:
            m_new = jnp.maximum(m, row_max)
            out = jnp.exp2(m - m_new) * out + _dot(jnp.exp2(s - m_new).astype(BF16), v)
            m = m_new
    out = out[:, :HEAD_DIM] / out[:, HEAD_DIM:]
    tq = q.shape[0]
    for g in range(Q_PER_KV):
        o_ref[:, g * HEAD_DIM:(g + 1) * HEAD_DIM] = out[g * tq:(g + 1) * tq, :].astype(BF16)


def _attention(q, segments, tq):
    b, s, _ = q.shape
    gw = Q_PER_KV * HEAD_DIM
    in_specs = [pl.BlockSpec((None, tq, gw), lambda bi, kh, t: (bi, t, kh))]
    args = [q]
    chunks = []
    for seg, (k, v) in enumerate(segments):
        n_keys = k.shape[1]
        in_specs += [pl.BlockSpec((None, n_keys, HEAD_DIM), lambda bi, kh, t: (bi, 0, kh)),
                     pl.BlockSpec((None, n_keys, 2 * HEAD_DIM), lambda bi, kh, t: (bi, 0, kh))]
        args += [k, v]
        step = min(ATTN_KEY_CHUNK, n_keys)
        chunks += [(seg, lo, step) for lo in range(0, n_keys, step)]
    return pl.pallas_call(
        functools.partial(_attention_kernel, tuple(chunks)),
        grid=(b, N_KV_HEADS, s // tq),
        in_specs=in_specs,
        out_specs=pl.BlockSpec((None, tq, gw), lambda bi, kh, t: (bi, t, kh)),
        out_shape=jax.ShapeDtypeStruct(q.shape, BF16),
        compiler_params=_cparams("parallel", "parallel", "parallel"),
        name=f"attention_{len(segments)}seg",
    )(*args)


def _mix_residual(z, h, wo_ref, mod_ref, gpost_ref):
    return h + mod_ref[5:6, :] * _rms(_dot(z, wo_ref[...]), gpost_ref[1:2, :])


def _attn_out_kernel(o_ref, wo_ref, h_ref, mod_ref, gpre_ref, gpost_ref, win_ref, wout_ref, out_ref):
    def mixed(rows):
        return _mix_residual(o_ref[rows, :], h_ref[rows, :], wo_ref, mod_ref, gpost_ref)

    _pipelined_swiglu(mixed, h_ref.shape[0], 2, mod_ref, gpre_ref, gpost_ref, win_ref, wout_ref, out_ref)


def _post_specs(w_o, m, w_in, w_out, layer, tm=TOKEN_TILE):
    return [_resident_slice(w_o.shape, (m,)), _tok_spec(D_MODEL, tm), _mod_spec(),
            _resident((3, D_MODEL)), _resident((3, D_MODEL)),
            _resident_slice(w_in.shape, (layer, 1)), _resident_slice(w_out.shape, (layer, 1))]


def _attn_out(o, w_o, m, h, mod, gpre, gpost, w_in, w_out, layer):
    nb, s, _ = h.shape
    return pl.pallas_call(
        _attn_out_kernel,
        grid=(nb, s // FFN_TILE),
        in_specs=[_tok_spec(D_MODEL, FFN_TILE)] + _post_specs(w_o, m, w_in, w_out, layer, FFN_TILE),
        out_specs=_tok_spec(D_MODEL, FFN_TILE),
        out_shape=jax.ShapeDtypeStruct(h.shape, F32),
        compiler_params=_cparams("parallel", "parallel"),
        name="attn_out_ffn",
    )(o, w_o, h, mod, gpre, gpost, w_in, w_out)


def _gla_out_kernel(of_ref, ob_ref, r_ref, og_ref, wo_ref, h_ref, mod_ref, gpre_ref, gpost_ref,
                    win_ref, wout_ref, out_ref):
    def mixed(rows):
        o = of_ref[rows, :].astype(F32) + ob_ref[rows, :].astype(F32)
        normed = jnp.concatenate(
            [_rms(o[:, hd * GLA_DV:(hd + 1) * GLA_DV], og_ref[:, hd * GLA_DV:(hd + 1) * GLA_DV])
             for hd in range(GLA_HEADS)], axis=-1)
        z = (normed * _silu(r_ref[rows, :].astype(F32))).astype(BF16)
        return _mix_residual(z, h_ref[rows, :], wo_ref, mod_ref, gpost_ref)

    _pipelined_swiglu(mixed, h_ref.shape[0], 2, mod_ref, gpre_ref, gpost_ref, win_ref, wout_ref, out_ref)


def _gla_out(o_f, o_b, r, o_gain, w_o, m, h, mod, gpre, gpost, w_in, w_out, layer):
    nb, s, _ = h.shape
    return pl.pallas_call(
        _gla_out_kernel,
        grid=(nb, s // FFN_TILE),
        in_specs=[_tok_spec(GLA_HV, FFN_TILE)] * 3 + [_resident((1, GLA_HV))]
        + _post_specs(w_o, m, w_in, w_out, layer, FFN_TILE),
        out_specs=_tok_spec(D_MODEL, FFN_TILE),
        out_shape=jax.ShapeDtypeStruct(h.shape, F32),
        compiler_params=_cparams("parallel", "parallel"),
        name="gla_out_ffn",
    )(o_f, o_b, r, o_gain, w_o, h, mod, gpre, gpost, w_in, w_out)


def _chunk_cumsum(g, reverse):
    rows, width = g.shape
    pos = lax.broadcasted_iota(jnp.int32, g.shape, 0) % GLA_CHUNK
    x = g
    for step in (1, 2, 4):
        if reverse:
            x = x + jnp.where(pos < GLA_CHUNK - step, pltpu.roll(x, rows - step, 0), 0.0)
        else:
            x = x + jnp.where(pos >= step, pltpu.roll(x, step, 0), 0.0)
    x = x.reshape(rows // GLA_CHUNK, GLA_CHUNK, width)
    for step in (8, 16, 32):
        keep = GLA_CHUNK - step
        if reverse:
            x = jnp.concatenate([x[:, :keep] + x[:, step:], x[:, keep:]], axis=1)
        else:
            x = jnp.concatenate([x[:, :step], x[:, step:] + x[:, :keep]], axis=1)
    return x.reshape(rows, width)


def _gla_prep_kernel(h_ref, mod_ref, gpre_ref, win_ref, wa1_ref, wa2_ref, ba_ref,
                     qdf_ref, kif_ref, qdb_ref, kib_ref, v_ref, r_ref, dec_ref, cum_ref):
    n_chunks = SUB_ROWS // GLA_CHUNK
    outs = ((qdf_ref, kif_ref), (qdb_ref, kib_ref))
    for rows in _sub_blocks(h_ref.shape[0]):
        u = _modulated(h_ref[rows, :], mod_ref, gpre_ref, 1)
        low = _dot(u, wa1_ref[...]).astype(BF16)
        pre_gates = _dot(low, wa2_ref[...]) + ba_ref[...]
        proj = _dot(u, win_ref[...])
        v_ref[rows, :] = proj[:, 2 * GLA_HK:2 * GLA_HK + GLA_HV].astype(BF16)
        r_ref[rows, :] = proj[:, 2 * GLA_HK + GLA_HV:].astype(BF16)
        chunk0 = rows.start // GLA_CHUNK
        for d, (qd_ref, ki_ref) in enumerate(outs):
            edge = rows.start + (GLA_CHUNK - 1 if d == 0 else 0)
            for hd in range(GLA_HEADS):
                lanes = slice(hd * GLA_DK, (hd + 1) * GLA_DK)
                gate_lanes = slice(d * GLA_HK + hd * GLA_DK, d * GLA_HK + (hd + 1) * GLA_DK)
                cum = _chunk_cumsum(_log_sigmoid(pre_gates[:, gate_lanes]) / GLA_GATE_TAU, reverse=(d == 1))
                qd_ref[rows, lanes] = (proj[:, lanes] * (GLA_DK ** -0.5) * jnp.exp(cum)).astype(BF16)
                ki_ref[rows, lanes] = (proj[:, GLA_HK + hd * GLA_DK:GLA_HK + (hd + 1) * GLA_DK]
                                       * jnp.exp(-cum)).astype(BF16)
                cum_ref[d * GLA_HEADS + hd, rows, :] = cum
                total = cum_ref[d * GLA_HEADS + hd, pl.ds(edge, n_chunks, stride=GLA_CHUNK), :]
                dec_ref[chunk0:chunk0 + n_chunks, gate_lanes] = jnp.exp(total)


def _gla_prep(h, mod, gpre, w_in, m, wa1, wa2, ba):
    nb, s, _ = h.shape
    nck = TOKEN_TILE // GLA_CHUNK
    half = jax.ShapeDtypeStruct((nb, s, GLA_HK), BF16)
    full = jax.ShapeDtypeStruct((nb, s, GLA_HV), BF16)
    return pl.pallas_call(
        _gla_prep_kernel,
        grid=(nb, s // TOKEN_TILE),
        in_specs=[_tok_spec(D_MODEL), _mod_spec(), _resident((3, D_MODEL)),
                  _resident_slice(w_in.shape, (m,)), _resident(wa1.shape), _resident(wa2.shape),
                  _resident(ba.shape)],
        out_specs=[_tok_spec(GLA_HK)] * 4 + [_tok_spec(GLA_HV)] * 2
        + [pl.BlockSpec((None, nck, 2 * GLA_HK), lambda b, t: (b, t, 0))],
        out_shape=[half] * 4 + [full] * 2
        + [jax.ShapeDtypeStruct((nb, s // GLA_CHUNK, 2 * GLA_HK), F32)],
        scratch_shapes=[pltpu.VMEM((2 * GLA_HEADS, TOKEN_TILE, GLA_DK), F32)],
        compiler_params=_cparams("parallel", "parallel"),
        name="gla_prep",
    )(h, mod, gpre, w_in, wa1, wa2, ba)


def _gla_scan_kernel(n_blocks, qdf_ref, kif_ref, vf_ref, qdb_ref, kib_ref, vb_ref,
                     dec_ref, s0_ref, of_ref, ob_ref, st_ref):
    t = pl.program_id(1)

    @pl.when(t == 0)
    def _():
        st_ref[...] = s0_ref[...]

    n_chunks = qdf_ref.shape[0] // GLA_CHUNK
    row = lax.broadcasted_iota(jnp.int32, (GLA_CHUNK, GLA_CHUNK), 0)
    col = lax.broadcasted_iota(jnp.int32, (GLA_CHUNK, GLA_CHUNK), 1)
    dirs = ((qdf_ref, kif_ref, vf_ref, of_ref, col <= row),
            (qdb_ref, kib_ref, vb_ref, ob_ref, col >= row))
    for d, (qd_ref, ki_ref, v_ref, o_ref, mask) in enumerate(dirs):
        block = t if d == 0 else n_blocks - 1 - t
        for step in range(n_chunks):
            c = step if d == 0 else n_chunks - 1 - step
            rows = slice(c * GLA_CHUNK, (c + 1) * GLA_CHUNK)
            decay = dec_ref[pl.ds(block * n_chunks + c, 1), :]
            for hd in range(GLA_HEADS):
                kl = slice(hd * GLA_DK, (hd + 1) * GLA_DK)
                vl = slice(hd * GLA_DV, (hd + 1) * GLA_DV)
                qd, ki, vv = qd_ref[rows, kl], ki_ref[rows, kl], v_ref[rows, vl]
                state = st_ref[d * GLA_HEADS + hd]
                scores = jnp.where(mask, _dot_nt(qd, ki), 0.0)
                o_ref[rows, vl] = (
                    _dot(scores.astype(BF16), vv) + _dot_nt(qd, state.astype(BF16))).astype(BF16)
                st_ref[d * GLA_HEADS + hd] = (
                    decay[:, d * GLA_HK + hd * GLA_DK:d * GLA_HK + (hd + 1) * GLA_DK]
                    * (state + _dot_tn(vv, ki)))


def _gla_scan(qdf, kif, qdb, kib, v, dec, state0):
    b, s, _ = v.shape
    tokens = min(SCAN_TOKENS, s)
    n_blocks = s // tokens
    fwd = lambda bi, t: (bi, t, 0)
    bwd = lambda bi, t: (bi, n_blocks - 1 - t, 0)
    hk = lambda m: pl.BlockSpec((None, tokens, GLA_HK), m)
    hv = lambda m: pl.BlockSpec((None, tokens, GLA_HV), m)
    st_spec = pl.BlockSpec((None, 2 * GLA_HEADS, GLA_DV, GLA_DK), lambda bi, t: (bi, 0, 0, 0))
    return pl.pallas_call(
        functools.partial(_gla_scan_kernel, n_blocks),
        grid=(b, n_blocks),
        in_specs=[hk(fwd), hk(fwd), hv(fwd), hk(bwd), hk(bwd), hv(bwd),
                  pl.BlockSpec((None, s // GLA_CHUNK, 2 * GLA_HK), lambda bi, t: (bi, 0, 0)),
                  st_spec],
        out_specs=[hv(fwd), hv(bwd), st_spec],
        out_shape=[jax.ShapeDtypeStruct((b, s, GLA_HV), BF16)] * 2
        + [jax.ShapeDtypeStruct(state0.shape, F32)],
        compiler_params=_cparams("parallel", "arbitrary"),
        name="gla_scan",
    )(qdf, kif, v, qdb, kib, v, dec, state0)


def _rope_tables(n_tok):
    pos = jnp.arange(n_tok, dtype=jnp.int32)
    inv = ROPE_THETA ** (-jnp.arange(0, ROPE_AXIS_DIM, 2, dtype=F32) / ROPE_AXIS_DIM)
    ang_r = (pos // GRID_W).astype(F32)[:, None] * inv[None, :]
    ang_c = (pos % GRID_W).astype(F32)[:, None] * inv[None, :]
    cos = jnp.concatenate([jnp.cos(ang_r), jnp.cos(ang_c)] * 2, axis=-1)
    sin = jnp.concatenate([-jnp.sin(ang_r), -jnp.sin(ang_c), jnp.sin(ang_r), jnp.sin(ang_c)], axis=-1)
    return cos, sin


def _rope_head_layout(a):
    lead = a.shape[:-1]
    a = a.reshape(lead + (-1, 2, 2, ROPE_HALF))
    return jnp.swapaxes(a, -2, -3).reshape(lead + (-1,))


def kernel(x, c, ctx, c_ctx, ada_w, ada_b, norm_pre, norm_post, ffn_w_in, ffn_w_out, attn_w_qkv, attn_q_gain, attn_k_gain, attn_w_o, gla_w_in, gla_wa1, gla_wa2, gla_ba, gla_o_gain, gla_w_o):
    b, n_tok, _ = x.shape
    n_ctx = ctx.shape[1]
    assert b + 1 <= MOD_ROWS and n_tok % FFN_TILE == 0 and (b * n_ctx) % FFN_TILE == 0

    cc = jnp.concatenate([c, c_ctx[None, :], jnp.zeros((MOD_ROWS - b - 1, D_MODEL), F32)], axis=0)
    mods = _ada(cc, ada_w, ada_b).reshape(DEPTH, MOD_ROWS, N_MOD, D_MODEL)
    rope = _rope_tables(n_tok)
    w_in, w_out = _to_bf16(ffn_w_in), _to_bf16(ffn_w_out)
    w_qkv, w_attn_o = _to_bf16(attn_w_qkv), _to_bf16(attn_w_o)
    w_gla_in, w_gla_o = _to_bf16(gla_w_in), _to_bf16(gla_w_o)
    qk_width = (N_HEADS + N_KV_HEADS) * HEAD_DIM
    w_qkv = jnp.concatenate([_rope_head_layout(w_qkv[..., :qk_width]), w_qkv[..., qk_width:]], axis=-1)

    def per_batch(a):
        return a.reshape(b, n_ctx, a.shape[-1])

    hx = x
    hc = ctx.reshape(1, b * n_ctx, D_MODEL)
    for i in range(DEPTH):
        last = i == DEPTH - 1
        mod_x, mod_c = mods[i, :b], mods[i, b:b + 1]
        gpre, gpost = norm_pre[i], norm_post[i]

        hx = _ffn_pre(hx, mod_x, gpre, gpost, w_in, w_out, i)
        hc = _ffn_pre(hc, mod_c, gpre, gpost, w_in, w_out, i)

        m = i // 2
        if i % 2 == 0:
            qg = _rope_head_layout(attn_q_gain[m])[None, :]
            kg = _rope_head_layout(attn_k_gain[m])[None, :]
            qx, kx, vx = _attn_prep(hx, mod_x, gpre, w_qkv, m, qg, kg, rope)
            qc, kc, vc = (per_batch(a) for a in _attn_prep(hc, mod_c, gpre, w_qkv, m, qg, kg, None))
            ox = _attention(qx, [(kc, vc), (kx, vx)], ATTN_TQ)
            hx = _attn_out(ox, w_attn_o, m, hx, mod_x, gpre, gpost, w_in, w_out, i)
            if not last:
                oc = _attention(qc, [(kc, vc)], n_ctx)
                hc = _attn_out(oc.reshape(hc.shape), w_attn_o, m, hc, mod_c, gpre, gpost, w_in, w_out, i)
        else:
            wa1 = jnp.concatenate([gla_wa1[m, 0], gla_wa1[m, 1]], axis=-1).astype(BF16)
            zeros = jnp.zeros((GLA_GATE_RANK, GLA_HK), F32)
            wa2 = jnp.concatenate([jnp.concatenate([gla_wa2[m, 0], zeros], axis=-1),
                                   jnp.concatenate([zeros, gla_wa2[m, 1]], axis=-1)], axis=0).astype(BF16)
            ba = gla_ba[m].reshape(1, 2 * GLA_HK)
            o_gain = gla_o_gain[m][None, :]
            px = _gla_prep(hx, mod_x, gpre, w_gla_in, m, wa1, wa2, ba)
            pc = _gla_prep(hc, mod_c, gpre, w_gla_in, m, wa1, wa2, ba)
            pc = [per_batch(a) for a in pc[:6]] + [pc[6].reshape(b, n_ctx // GLA_CHUNK, 2 * GLA_HK)]
            zero_state = jnp.zeros((b, 2 * GLA_HEADS, GLA_DV, GLA_DK), F32)
            oc_f, oc_b, ctx_state = _gla_scan(*pc[:5], pc[6], zero_state)
            ox_f, ox_b, _ = _gla_scan(*px[:5], px[6], ctx_state)
            hx = _gla_out(ox_f, ox_b, px[5], o_gain, w_gla_o, m, hx, mod_x, gpre, gpost, w_in, w_out, i)
            if not last:
                hc = _gla_out(oc_f.reshape(1, b * n_ctx, GLA_HV), oc_b.reshape(1, b * n_ctx, GLA_HV),
                              pc[5].reshape(1, b * n_ctx, GLA_HV), o_gain, w_gla_o, m, hc, mod_c,
                              gpre, gpost, w_in, w_out, i)
    return hx
```
